```python
import math, functools
import jax, jax.numpy as jnp
from jax import lax
import numpy as np

D_MODEL = 1024
BATCH = 8
SEQ = 2048
DEPTH = 2
DEC_BATCH = 32
DEC_SEQ = 8
PAST_LEN = 16384
PAGE_SIZE = 128

F32 = jnp.float32
EPS = 1e-6
N_AB = (DEPTH + 1) // 2
N_CD = DEPTH // 2
SSM_HEAD_DIM = 64
SSM_HEADS = D_MODEL // SSM_HEAD_DIM
SSM_INNER = SSM_HEADS * SSM_HEAD_DIM
SSM_GROUPS = 2
SSM_STATE = 128
SSM_CONV = 4
SSM_CONV_CH = SSM_INNER + 2 * SSM_GROUPS * SSM_STATE
SSM_CHUNK = 64
GLA_HEADS = 4
GLA_DK = D_MODEL // 2 // GLA_HEADS
GLA_DV = D_MODEL // GLA_HEADS
GLA_RANK = 16
GLA_GATE_NORM = 16.0
GLA_CHUNK = 32
FOX_HEAD_DIM = 64
FOX_HEADS = D_MODEL // FOX_HEAD_DIM
FOX_BLOCK = 128
FOX_BIAS_LO = 2.0
FOX_BIAS_HI = 10.0
CONF_CH = D_MODEL // 2
CONF_WIDTH = 31
D_FF = 4 * D_MODEL

AB_SIZES = (SSM_INNER, SSM_CONV_CH, SSM_HEADS, GLA_HEADS * GLA_DK, GLA_HEADS * GLA_DK,
            GLA_HEADS * GLA_DV, GLA_HEADS * GLA_DV, GLA_RANK)
AB_IN = sum(AB_SIZES)
AB_MIX = SSM_INNER + GLA_HEADS * GLA_DV
CD_SIZES = (FOX_HEADS * FOX_HEAD_DIM, FOX_HEADS * FOX_HEAD_DIM, FOX_HEADS * FOX_HEAD_DIM, FOX_HEADS, 2 * CONF_CH)
CD_IN = sum(CD_SIZES)
CD_MIX = FOX_HEADS * FOX_HEAD_DIM + CONF_CH

kernel_name = 'hybrid_ssd_gla_fox_conformer_step'


def split_cols(x, sizes):
    parts, off = [], 0
    for s in sizes:
        parts.append(x[..., off:off + s])
        off += s
    return parts


def rms_norm(x, g):
    xf = x.astype(F32)
    y = xf * lax.rsqrt(jnp.mean(xf * xf, axis=-1, keepdims=True) + EPS)
    return (y * g.astype(F32)).astype(x.dtype)


def layer_norm(x, g, b):
    xf = x.astype(F32)
    xc = xf - jnp.mean(xf, axis=-1, keepdims=True)
    var = jnp.mean(xc * xc, axis=-1, keepdims=True)
    return (xc * lax.rsqrt(var + EPS) * g.astype(F32) + b.astype(F32)).astype(x.dtype)


def causal_dwconv(u, buf, w, b):
    width, ch = w.shape
    full = jnp.concatenate([buf.astype(u.dtype), u], axis=1)
    y = lax.conv_general_dilated(full, w[:, None, :].astype(u.dtype), window_strides=(1,), padding='VALID',
                                 dimension_numbers=('NWC', 'WIO', 'NWC'), feature_group_count=ch)
    return y + b.astype(u.dtype), full[:, full.shape[1] - (width - 1):]


def sq_relu_mlp(x, w_up, w_down):
    return jnp.square(jax.nn.relu(x @ w_up)) @ w_down


def ssd_scan(xh, a, bm, cm, h0):
    bsz, T, H, P = xh.shape
    G, N = bm.shape[2], bm.shape[3]
    R = H // G
    L = math.gcd(T, SSM_CHUNK)
    nc = T // L
    xc = xh.reshape(bsz, nc, L, G, R, P)
    ac = jnp.cumsum(a.reshape(bsz, nc, L, G, R), axis=2)
    bc = bm.reshape(bsz, nc, L, G, N)
    cc = cm.reshape(bsz, nc, L, G, N)
    causal = jnp.tril(jnp.ones((L, L), bool))
    seg = ac[:, :, :, None] - ac[:, :, None]
    decay = jnp.exp(jnp.where(causal[None, None, :, :, None, None], seg, -jnp.inf))
    cb = jnp.einsum('bclgn,bcsgn->bclsg', cc, bc)
    y_diag = jnp.einsum('bclsg,bclsgr,bcsgrp->bclgrp', cb, decay, xc)
    to_end = jnp.exp(ac[:, :, -1:] - ac)
    states = jnp.einsum('bcsgn,bcsgr,bcsgrp->bcgrpn', bc, to_end, xc)
    chunk_decay = jnp.exp(ac[:, :, -1])

    def step(h, inp):
        d, s = inp
        return d[..., None, None] * h + s, h

    h_last, h_in = lax.scan(step, h0.reshape(bsz, G, R, P, N),
                            (chunk_decay.transpose(1, 0, 2, 3), states.transpose(1, 0, 2, 3, 4, 5)))
    y_off = jnp.einsum('bclgn,bclgr,bcgrpn->bclgrp', cc, jnp.exp(ac), h_in.transpose(1, 0, 2, 3, 4, 5))
    return (y_diag + y_off).reshape(bsz, T, H, P), h_last.reshape(bsz, H, P, N)


def gla_scan(q, k, v, logf, s0):
    bsz, T, H, K = q.shape
    L = math.gcd(T, GLA_CHUNK)
    nc = T // L
    rs = lambda t: t.reshape(bsz, nc, L, H, t.shape[-1])
    q, k, v = rs(q), rs(k), rs(v)
    bcum = jnp.cumsum(rs(logf), axis=2)
    q_dec = q * jnp.exp(bcum)
    k_inv = k * jnp.exp(-bcum)
    k_end = k * jnp.exp(bcum[:, :, -1:] - bcum)
    causal = jnp.tril(jnp.ones((L, L), bool))
    att = jnp.where(causal, jnp.einsum('bclhk,bcshk->bchls', q_dec, k_inv), 0.0)
    o_intra = jnp.einsum('bchls,bcshv->bclhv', att, v)
    upd = jnp.einsum('bcshk,bcshv->bchkv', k_end, v)
    dec = jnp.exp(bcum[:, :, -1])

    def step(s, inp):
        d, u = inp
        return d[..., None] * s + u, s

    s_last, s_in = lax.scan(step, s0, (dec.transpose(1, 0, 2, 3), upd.transpose(1, 0, 2, 3, 4)))
    o_inter = jnp.einsum('bclhk,bchkv->bclhv', q_dec, s_in.transpose(1, 0, 2, 3, 4))
    return (o_intra + o_inter).reshape(bsz, T, H, v.shape[-1]), s_last


def mixer_ab(xn, h0, conv_buf, s0, w_in, conv_w, conv_b, dt_bias, a_log, d_skip, ssm_norm_g,
             gate_w2, gate_b, gla_norm_g, w_out):
    bsz, T, _ = xn.shape
    z, xbc, dt_raw, q, k, v, g, gate_lr = split_cols(xn @ w_in, AB_SIZES)
    xbc, new_conv = causal_dwconv(xbc, conv_buf, conv_w, conv_b)
    xbc = jax.nn.silu(xbc.astype(F32))
    xs, bm, cm = split_cols(xbc, (SSM_INNER, SSM_GROUPS * SSM_STATE, SSM_GROUPS * SSM_STATE))
    xs = xs.reshape(bsz, T, SSM_HEADS, SSM_HEAD_DIM)
    dt = jax.nn.softplus(dt_raw.astype(F32) + dt_bias.astype(F32))
    a = -jnp.exp(a_log.astype(F32)) * dt
    y, h_new = ssd_scan(xs * dt[..., None], a, bm.reshape(bsz, T, SSM_GROUPS, SSM_STATE),
                        cm.reshape(bsz, T, SSM_GROUPS, SSM_STATE), h0.astype(F32))
    y = y + d_skip.astype(F32)[:, None] * xs
    y = rms_norm(y.reshape(bsz, T, SSM_INNER) * jax.nn.silu(z.astype(F32)), ssm_norm_g)
    logf = jax.nn.log_sigmoid(gate_lr.astype(F32) @ gate_w2.astype(F32) + gate_b.astype(F32)) / GLA_GATE_NORM
    heads = lambda t, d: t.astype(F32).reshape(bsz, T, GLA_HEADS, d)
    o, s_new = gla_scan(heads(q, GLA_DK) * GLA_DK ** -0.5, heads(k, GLA_DK), heads(v, GLA_DV),
                        heads(logf, GLA_DK), s0.astype(F32))
    o = rms_norm(o, gla_norm_g).reshape(bsz, T, GLA_HEADS * GLA_DV) * jax.nn.silu(g.astype(F32))
    out = jnp.concatenate([y, o], axis=-1).astype(xn.dtype) @ w_out
    return out, h_new.astype(xn.dtype), new_conv, s_new.astype(xn.dtype)


def fox_prompt_attend(q, k, v, logf):
    bsz, T, H, Dh = q.shape
    c = jnp.cumsum(logf, axis=1).transpose(0, 2, 1)
    scale = Dh ** -0.5
    kpos = jnp.arange(T)

    def block(i):
        start = i * FOX_BLOCK
        qb = lax.dynamic_slice_in_dim(q, start, FOX_BLOCK, axis=1)
        cq = lax.dynamic_slice_in_dim(c, start, FOX_BLOCK, axis=2)
        s = jnp.einsum('bqhd,bkhd->bhqk', qb, k).astype(F32) * scale + cq[..., None] - c[:, :, None, :]
        qpos = start + jnp.arange(FOX_BLOCK)
        s = jnp.where(kpos[None, :] <= qpos[:, None], s, -jnp.inf)
        p = jax.nn.softmax(s, axis=-1)
        return jnp.einsum('bhqk,bkhd->bqhd', p.astype(v.dtype), v)

    o = lax.map(block, jnp.arange(T // FOX_BLOCK))
    return o.transpose(1, 0, 2, 3, 4).reshape(bsz, T, H, Dh)


def fox_sample_attend(q, k, v, logf, cache_k, cache_v, cache_logf, page_table, layer):
    bsz, T, H, Dh = q.shape
    n_pages = page_table.shape[1]
    qf = q.astype(F32) * Dh ** -0.5
    cn = jnp.cumsum(logf, axis=1).transpose(0, 2, 1)
    s_new = jnp.einsum('bthd,bshd->bhts', qf, k.astype(F32)) + cn[..., :, None] - cn[..., None, :]
    s_new = jnp.where(jnp.tril(jnp.ones((T, T), bool)), s_new, -jnp.inf)
    m = jnp.max(s_new, axis=-1)
    p = jnp.exp(s_new - m[..., None])
    l = jnp.sum(p, axis=-1)
    acc = jnp.einsum('bhts,bshd->bhtd', p, v.astype(F32))
    lf_past = cache_logf[layer][page_table].reshape(bsz, n_pages * PAGE_SIZE, H).astype(F32)
    suffix = lax.cumsum(lf_past, axis=1, reverse=True) - lf_past
    suffix = suffix.reshape(bsz, n_pages, PAGE_SIZE, H).transpose(1, 0, 3, 2)

    def page_step(carry, inp):
        m, l, acc = carry
        phys, suf = inp
        kp = cache_k[layer][phys].astype(F32)
        vp = cache_v[layer][phys].astype(F32)
        s = jnp.einsum('bthd,bshd->bhts', qf, kp) + cn[..., None] + suf[:, :, None, :]
        m_new = jnp.maximum(m, jnp.max(s, axis=-1))
        corr = jnp.exp(m - m_new)
        p = jnp.exp(s - m_new[..., None])
        acc = acc * corr[..., None] + jnp.einsum('bhts,bshd->bhtd', p, vp)
        return (m_new, l * corr + jnp.sum(p, axis=-1), acc), None

    (m, l, acc), _ = lax.scan(page_step, (m, l, acc), (page_table.T, suffix))
    return (acc / l[..., None]).transpose(0, 2, 1, 3).astype(q.dtype)


def mixer_cd(xn, conf_buf, attend, w_in, b_f, conv_w, conv_b, ln_g, ln_b, w_out):
    bsz, T, _ = xn.shape
    q, k, v, f_raw, u = split_cols(xn @ w_in, CD_SIZES)
    heads = lambda t: t.reshape(bsz, T, FOX_HEADS, FOX_HEAD_DIM)
    q, k, v = heads(q), heads(k), heads(v)
    logf = jax.nn.log_sigmoid(f_raw.astype(F32) + b_f.astype(F32))
    o = attend(q, k, v, logf).reshape(bsz, T, FOX_HEADS * FOX_HEAD_DIM)
    ua, ub = jnp.split(u, 2, axis=-1)
    c, new_buf = causal_dwconv(ua * jax.nn.sigmoid(ub), conf_buf, conv_w, conv_b)
    c = jax.nn.silu(layer_norm(c, ln_g, ln_b))
    out = jnp.concatenate([o.astype(xn.dtype), c.astype(xn.dtype)], axis=-1) @ w_out
    return out, k, v, logf.astype(xn.dtype), new_buf


def setup_inputs(seed: int = 0) -> dict:
    key = jax.random.key(seed)
    ks = iter(jax.random.split(key, 48))
    nrm = lambda shape, scale=1.0: scale * jax.random.normal(next(ks), shape, F32)
    n_pages = PAST_LEN // PAGE_SIZE
    n_pool = (DEC_BATCH * n_pages * 5) // 4
    page_table = jax.random.permutation(next(ks), n_pool)[:DEC_BATCH * n_pages].reshape(DEC_BATCH, n_pages).astype(jnp.int32)
    dt0 = jnp.exp(jax.random.uniform(next(ks), (N_AB, SSM_HEADS), F32, math.log(1e-3), math.log(1e-1)))
    dt_bias = dt0 + jnp.log(-jnp.expm1(-dt0))
    a_log = jnp.log(jax.random.uniform(next(ks), (N_AB, SSM_HEADS), F32, 1.0, 16.0))
    fox_head_bias = jnp.linspace(FOX_BIAS_LO, FOX_BIAS_HI, FOX_HEADS, dtype=F32)
    return {
        'x_prompt': nrm((BATCH, SEQ, D_MODEL)),
        'x_sample': nrm((DEC_BATCH, DEC_SEQ, D_MODEL)),
        'cache_fox_k': nrm((N_CD, n_pool, PAGE_SIZE, FOX_HEADS, FOX_HEAD_DIM)),
        'cache_fox_v': nrm((N_CD, n_pool, PAGE_SIZE, FOX_HEADS, FOX_HEAD_DIM)),
        'cache_fox_logf': jax.nn.log_sigmoid(fox_head_bias + nrm((N_CD, n_pool, PAGE_SIZE, FOX_HEADS))),
        'page_table': page_table,
        'state_ssm': nrm((N_AB, DEC_BATCH, SSM_HEADS, SSM_HEAD_DIM, SSM_STATE), 0.1),
        'state_ssm_conv': nrm((N_AB, DEC_BATCH, SSM_CONV - 1, SSM_CONV_CH)),
        'state_gla': nrm((N_AB, DEC_BATCH, GLA_HEADS, GLA_DK, GLA_DV), 0.1),
        'state_conf_conv': nrm((N_CD, DEC_BATCH, CONF_WIDTH - 1, CONF_CH), 0.5),
        'g_mix': 1.0 + nrm((DEPTH, D_MODEL), 0.01),
        'g_mlp': 1.0 + nrm((DEPTH, D_MODEL), 0.01),
        'g_final': 1.0 + nrm((D_MODEL,), 0.01),
        'w_in_ab': nrm((N_AB, D_MODEL, AB_IN), D_MODEL ** -0.5),
        'ssm_conv_w': nrm((N_AB, SSM_CONV, SSM_CONV_CH), SSM_CONV ** -0.5),
        'ssm_conv_b': nrm((N_AB, SSM_CONV_CH), 0.01),
        'ssm_dt_bias': dt_bias,
        'ssm_a_log': a_log,
        'ssm_d': 1.0 + nrm((N_AB, SSM_HEADS), 0.01),
        'ssm_norm_g': 1.0 + nrm((N_AB, SSM_INNER), 0.01),
        'gla_gate_w2': nrm((N_AB, GLA_RANK, GLA_HEADS * GLA_DK), GLA_RANK ** -0.5),
        'gla_gate_b': nrm((N_AB, GLA_HEADS * GLA_DK), 0.01),
        'gla_norm_g': 1.0 + nrm((N_AB, GLA_DV), 0.01),
        'w_out_ab': nrm((N_AB, AB_MIX, D_MODEL), AB_MIX ** -0.5),
        'w_in_cd': nrm((N_CD, D_MODEL, CD_IN), D_MODEL ** -0.5),
        'fox_b_f': fox_head_bias + nrm((N_CD, FOX_HEADS), 0.1),
        'conf_conv_w': nrm((N_CD, CONF_WIDTH, CONF_CH), CONF_WIDTH ** -0.5),
        'conf_conv_b': nrm((N_CD, CONF_CH), 0.01),
        'conf_ln_g': 1.0 + nrm((N_CD, CONF_CH), 0.01),
        'conf_ln_b': nrm((N_CD, CONF_CH), 0.01),
        'w_out_cd': nrm((N_CD, CD_MIX, D_MODEL), CD_MIX ** -0.5),
        'w_mlp_up': nrm((DEPTH, D_MODEL, D_FF), D_MODEL ** -0.5),
        'w_mlp_down': nrm((DEPTH, D_FF, D_MODEL), D_FF ** -0.5),
    }


def reference(x_prompt, x_sample, cache_fox_k, cache_fox_v, cache_fox_logf, page_table,
              state_ssm, state_ssm_conv, state_gla, state_conf_conv,
              g_mix, g_mlp, g_final, w_in_ab, ssm_conv_w, ssm_conv_b, ssm_dt_bias, ssm_a_log, ssm_d, ssm_norm_g,
              gla_gate_w2, gla_gate_b, gla_norm_g, w_out_ab,
              w_in_cd, fox_b_f, conf_conv_w, conf_conv_b, conf_ln_g, conf_ln_b, w_out_cd,
              w_mlp_up, w_mlp_down):
    def run_trunk(x, sample):
        bsz = x.shape[0]
        ssm_l, ssmc_l, gla_l, k_l, v_l, lf_l, conf_l = [], [], [], [], [], [], []
        for i in range(DEPTH):
            j = i // 2
            xn = rms_norm(x, g_mix[i])
            if i % 2 == 0:
                if sample:
                    h0, cbuf, s0 = state_ssm[j], state_ssm_conv[j], state_gla[j]
                else:
                    h0 = jnp.zeros((bsz, SSM_HEADS, SSM_HEAD_DIM, SSM_STATE), x.dtype)
                    cbuf = jnp.zeros((bsz, SSM_CONV - 1, SSM_CONV_CH), x.dtype)
                    s0 = jnp.zeros((bsz, GLA_HEADS, GLA_DK, GLA_DV), x.dtype)
                out, h, cb, s = mixer_ab(xn, h0, cbuf, s0, w_in_ab[j], ssm_conv_w[j], ssm_conv_b[j], ssm_dt_bias[j],
                                         ssm_a_log[j], ssm_d[j], ssm_norm_g[j], gla_gate_w2[j], gla_gate_b[j],
                                         gla_norm_g[j], w_out_ab[j])
                ssm_l.append(h)
                ssmc_l.append(cb)
                gla_l.append(s)
            else:
                if sample:
                    buf = state_conf_conv[j]
                    attend = functools.partial(fox_sample_attend, cache_k=cache_fox_k, cache_v=cache_fox_v,
                                               cache_logf=cache_fox_logf, page_table=page_table, layer=j)
                else:
                    buf = jnp.zeros((bsz, CONF_WIDTH - 1, CONF_CH), x.dtype)
                    attend = fox_prompt_attend
                out, k, v, lf, nb = mixer_cd(xn, buf, attend, w_in_cd[j], fox_b_f[j], conf_conv_w[j], conf_conv_b[j],
                                             conf_ln_g[j], conf_ln_b[j], w_out_cd[j])
                k_l.append(k)
                v_l.append(v)
                lf_l.append(lf)
                conf_l.append(nb)
            x = x + out
            x = x + sq_relu_mlp(rms_norm(x, g_mlp[i]), w_mlp_up[i], w_mlp_down[i])
        return (rms_norm(x, g_final), jnp.stack(ssm_l), jnp.stack(ssmc_l), jnp.stack(gla_l),
                jnp.stack(k_l), jnp.stack(v_l), jnp.stack(lf_l), jnp.stack(conf_l))

    y_prompt, p_ssm, p_ssm_conv, p_gla, p_k, p_v, p_lf, p_conf = run_trunk(x_prompt, False)
    y_sample, s_ssm, s_ssm_conv, s_gla, s_k, s_v, s_lf, s_conf = run_trunk(x_sample, True)
    return (y_prompt, y_sample, p_ssm, p_ssm_conv, p_gla, p_k, p_v, p_lf, p_conf,
            s_ssm, s_ssm_conv, s_gla, s_k, s_v, s_lf, s_conf)
```

```python
import functools

import jax
import jax.numpy as jnp
from jax import lax
from jax.experimental import pallas as pl
from jax.experimental.pallas import tpu as pltpu

F32 = jnp.float32
BF16 = jnp.bfloat16
EPS = 1e-6

LANES = 128
MIB = 1024 * 1024

D_MODEL = 1024
D_FF = 4 * D_MODEL
SSM_HEADS = 16
SSM_HEAD_DIM = 64
SSM_INNER = SSM_HEADS * SSM_HEAD_DIM
SSM_GROUPS = 2
SSM_STATE = 128
SSM_CONV = 4
SSM_CONV_CH = SSM_INNER + 2 * SSM_GROUPS * SSM_STATE
GLA_HEADS = 4
GLA_DK = 128
GLA_DV = 256
GLA_RANK = 16
GLA_GATE_NORM = 16.0
FOX_HEADS = 16
FOX_HEAD_DIM = 64
CONF_CH = 512
CONF_WIDTH = 31
PAGE_SIZE = 128

AB_SPLIT = (SSM_INNER, SSM_CONV_CH, SSM_HEADS, GLA_HEADS * GLA_DK, GLA_HEADS * GLA_DK,
            GLA_HEADS * GLA_DV, GLA_HEADS * GLA_DV, GLA_RANK)
CD_SPLIT = (FOX_HEADS * FOX_HEAD_DIM,) * 3 + (FOX_HEADS, 2 * CONF_CH)
AB_SECTIONS = (SSM_INNER, SSM_CONV_CH, GLA_HEADS * GLA_DK, GLA_HEADS * GLA_DK, GLA_HEADS * GLA_DV,
               GLA_HEADS * GLA_DV, LANES, LANES)
CD_SECTIONS = (1024, 1024, 1024, 2 * CONF_CH, LANES)

SSD_CHUNK = 128
GLA_CHUNK = 64
GATE_CHUNK = 128
FOX_TQ = 256
FOX_TK = 256
CONF_CHUNK = 256
CONF_PAD = 32
SSD_PAD = 8
FF_CHUNK = 1024
PAGES_PER_STEP = 8


def _dot(a, b):
    return jnp.dot(a, b, preferred_element_type=F32)


def _dot_nt(a, b):
    return lax.dot_general(a, b, (((1,), (1,)), ((), ())), preferred_element_type=F32)


def _dot_tn(a, b):
    return lax.dot_general(a, b, (((0,), (0,)), ((), ())), preferred_element_type=F32)


def _split3(x):
    hi = x.astype(BF16)
    r1 = x - hi.astype(F32)
    mid = r1.astype(BF16)
    lo = (r1 - mid.astype(F32)).astype(BF16)
    return hi, mid, lo


def _dot_exact_l(m, x):
    hi, mid, lo = _split3(x)
    return _dot(m, lo) + _dot(m, mid) + _dot(m, hi)


def _dot_exact_r(x, m):
    hi, mid, lo = _split3(x)
    return _dot(lo, m) + _dot(mid, m) + _dot(hi, m)


def _sigmoid(x):
    return 1.0 / (1.0 + jnp.exp(-x))


def _silu(x):
    return x * _sigmoid(x)


def _softplus(x):
    return jnp.maximum(x, 0.0) + jnp.log1p(jnp.exp(-jnp.abs(x)))


def _log_sigmoid(x):
    return jnp.minimum(x, 0.0) - jnp.log1p(jnp.exp(-jnp.abs(x)))


def _rms(x, g):
    return x * lax.rsqrt(jnp.mean(x * x, axis=-1, keepdims=True) + EPS) * g


def _iota(shape, dim):
    return lax.broadcasted_iota(jnp.int32, shape, dim)


def _tril(n):
    return _iota((n, n), 1) <= _iota((n, n), 0)


def _params(semantics, vmem_mib):
    return pltpu.CompilerParams(dimension_semantics=semantics, vmem_limit_bytes=vmem_mib * MIB)


def _resident(shape):
    nd = len(shape)
    return pl.BlockSpec(shape, lambda *_: (0,) * nd, pipeline_mode=pl.Buffered(1))


def _norm_proj_kernel(x_ref, g_ref, w_ref, *out_refs, sections):
    xb = _rms(x_ref[...], g_ref[...]).astype(BF16)
    for o_ref, (off, width) in zip(out_refs, sections):
        o_ref[...] = _dot(xb, w_ref[:, off:off + width])


def _norm_proj(x, g, w, widths, tm):
    n = x.shape[0]
    offs = [sum(widths[:i]) for i in range(len(widths))]
    kern = functools.partial(_norm_proj_kernel, sections=tuple(zip(offs, widths)))
    wtot = w.shape[1]
    vmem = (2 * tm * D_MODEL * 4 + D_MODEL * wtot * 2 + 3 * tm * wtot * 4) // MIB + 4
    return pl.pallas_call(
        kern, grid=(n // tm,),
        in_specs=[pl.BlockSpec((tm, D_MODEL), lambda i: (i, 0)), _resident((1, D_MODEL)), _resident(w.shape)],
        out_specs=[pl.BlockSpec((tm, s), lambda i: (i, 0)) for s in widths],
        out_shape=[jax.ShapeDtypeStruct((n, s), F32) for s in widths],
        compiler_params=_params(("arbitrary",), vmem), name="norm_proj")(x, g, w)


def _ssd_kernel(z_ref, xbc_ref, dt_ref, conv0_ref, h0_ref, cw_ref, cb_ref, dtb_ref, alog_ref, dsk_ref, ng_ref,
                e16_ref, y_ref, hout_ref, convout_ref, xbuf, h_scr, y_scr, *, n_chunks, t_valid):
    c = pl.program_id(1)
    L = SSD_CHUNK
    P0 = SSD_PAD - (SSM_CONV - 1)
    last_valid = t_valid - (n_chunks - 1) * L

    @pl.when(c == 0)
    def _():
        xbuf[P0:SSD_PAD, :] = conv0_ref[...]
        h_scr[...] = h0_ref[...]

    xbuf[SSD_PAD:SSD_PAD + L, :] = xbc_ref[...]
    cw = cw_ref[...]
    conv = cb_ref[...] + xbuf[P0:P0 + L, :] * cw[0:1, :]
    for j in range(1, SSM_CONV):
        conv = conv + xbuf[P0 + j:P0 + j + L, :] * cw[j:j + 1, :]
    act = _silu(conv)
    xs = act[:, :SSM_INNER]
    bm_b = act[:, SSM_INNER:SSM_INNER + SSM_GROUPS * SSM_STATE].astype(BF16)
    cm_b = act[:, SSM_INNER + SSM_GROUPS * SSM_STATE:].astype(BF16)

    row = _iota((L, LANES), 0)
    lane = _iota((L, LANES), 1)
    live = (lane < SSM_HEADS) & (c * L + row < t_valid)
    dt = jnp.where(live, _softplus(dt_ref[...] + dtb_ref[...]), 0.0)
    a = -jnp.exp(alog_ref[...]) * dt
    causal = _tril(L)
    tri = jnp.where(causal, 1.0, 0.0).astype(BF16)
    ac = _dot_exact_l(tri, a)
    e16 = e16_ref[...]
    ac_x = _dot_exact_r(ac, e16)
    dt_x = _dot_exact_r(dt, e16)
    ac_t = ac.T
    ac_last = ac[L - 1:L, :]
    ac_last_x = ac_x[L - 1:L, :]
    to_end_x = jnp.exp(ac_last_x - ac_x)
    eac_x = jnp.exp(ac_x)
    xdt = xs * dt_x
    xdt_b = xdt.astype(BF16)
    xend_b = (xdt * to_end_x).astype(BF16)
    lane_lo = lane < SSM_HEAD_DIM
    row_lo = _iota((LANES, LANES), 0) < SSM_HEAD_DIM
    heads_per_group = SSM_HEADS // SSM_GROUPS
    for g in range(SSM_GROUPS):
        bg = bm_b[:, g * SSM_STATE:(g + 1) * SSM_STATE]
        cg = cm_b[:, g * SSM_STATE:(g + 1) * SSM_STATE]
        cb = _dot_nt(cg, bg)
        for p in range(g * heads_per_group // 2, (g + 1) * heads_per_group // 2):
            sl = slice(p * LANES, (p + 1) * LANES)
            ys = []
            for j in range(2):
                h = 2 * p + j
                seg = ac[:, h:h + 1] - ac_t[h:h + 1, :]
                m = (cb * jnp.exp(jnp.where(causal, seg, -jnp.inf))).astype(BF16)
                ys.append(_dot(m, xdt_b[:, sl]))
            y_diag = jnp.where(lane_lo, ys[0], ys[1])
            s_old = h_scr[sl, :]
            y_off = _dot_nt(cg, s_old.astype(BF16)) * eac_x[:, sl]
            dec = jnp.exp(jnp.where(row_lo, ac_last[:, 2 * p:2 * p + 1], ac_last[:, 2 * p + 1:2 * p + 2]))
            h_scr[sl, :] = s_old * dec + _dot_tn(xend_b[:, sl], bg)
            y_scr[:, sl] = y_diag + y_off + dsk_ref[:, sl] * xs[:, sl]

    y_ref[...] = _rms(y_scr[...] * _silu(z_ref[...]), ng_ref[...])

    @pl.when(c == n_chunks - 1)
    def _():
        hout_ref[...] = h_scr[...]
        convout_ref[...] = xbuf[P0 + last_valid:SSD_PAD + last_valid, :]

    tail = xbuf[P0 + L:SSD_PAD + L, :]
    xbuf[P0:SSD_PAD, :] = tail


def _ssd(z, xbc, dtp, conv0, h0, cw, cb, dtb, alog, dsk, ng, e16, t_valid):
    bsz, t_pad, _ = z.shape
    L = SSD_CHUNK
    nc = t_pad // L
    kern = functools.partial(_ssd_kernel, n_chunks=nc, t_valid=t_valid)
    tok = lambda w: pl.BlockSpec((None, L, w), lambda b, c: (b, c, 0))
    per_b = lambda s: pl.BlockSpec((None,) + s, lambda b, c: (b, 0, 0))
    return pl.pallas_call(
        kern, grid=(bsz, nc),
        in_specs=[tok(SSM_INNER), tok(SSM_CONV_CH), tok(LANES), per_b((SSM_CONV - 1, SSM_CONV_CH)),
                  per_b((SSM_INNER, SSM_STATE)), _resident(cw.shape), _resident(cb.shape), _resident(dtb.shape),
                  _resident(alog.shape), _resident(dsk.shape), _resident(ng.shape), _resident(e16.shape)],
        out_specs=[tok(SSM_INNER), per_b((SSM_INNER, SSM_STATE)), per_b((SSM_CONV - 1, SSM_CONV_CH))],
        out_shape=[jax.ShapeDtypeStruct((bsz, t_pad, SSM_INNER), F32),
                   jax.ShapeDtypeStruct((bsz, SSM_INNER, SSM_STATE), F32),
                   jax.ShapeDtypeStruct((bsz, SSM_CONV - 1, SSM_CONV_CH), F32)],
        scratch_shapes=[pltpu.VMEM((SSD_PAD + L, SSM_CONV_CH), F32), pltpu.VMEM((SSM_INNER, SSM_STATE), F32),
                        pltpu.VMEM((L, SSM_INNER), F32)],
        compiler_params=_params(("arbitrary", "arbitrary"), 40), name="ssd")(
            z, xbc, dtp, conv0, h0, cw, cb, dtb, alog, dsk, ng, e16)


def _gla_kernel(q_ref, k_ref, v_ref, g_ref, glr_ref, s0_ref, w2_ref, gb_ref, ng_ref, o_ref, sout_ref, s_scr,
                *, n_chunks, t_valid):
    c = pl.program_id(1)
    L = GLA_CHUNK
    width = GLA_HEADS * GLA_DK

    @pl.when(c == 0)
    def _():
        s_scr[...] = s0_ref[...]

    x = _dot(glr_ref[...].astype(BF16), w2_ref[...]) + gb_ref[...]
    logf = _log_sigmoid(x) * (1.0 / GLA_GATE_NORM)
    logf = jnp.where(c * L + _iota((L, width), 0) < t_valid, logf, 0.0)
    causal = _tril(L)
    tri = jnp.where(causal, 1.0, 0.0).astype(BF16)
    bcum = _dot_exact_l(tri, logf)
    bl = bcum[L - 1:L, :]
    q_dec = q_ref[...] * (GLA_DK ** -0.5) * jnp.exp(bcum)
    k = k_ref[...]
    k_inv = k * jnp.exp(-bcum)
    k_end = k * jnp.exp(bl - bcum)
    dec_row = jnp.exp(bl)
    for h in range(GLA_HEADS):
        ks = slice(h * GLA_DK, (h + 1) * GLA_DK)
        vs = slice(h * GLA_DV, (h + 1) * GLA_DV)
        qd = q_dec[:, ks].astype(BF16)
        vb = v_ref[:, vs].astype(BF16)
        att = jnp.where(causal, _dot_nt(qd, k_inv[:, ks].astype(BF16)), 0.0)
        s_old = s_scr[ks, :]
        o = _dot(att.astype(BF16), vb) + _dot(qd, s_old.astype(BF16))
        dcol = jnp.broadcast_to(dec_row[:, ks], (GLA_DK, GLA_DK)).T
        s_scr[ks, :] = s_old * jnp.concatenate([dcol, dcol], axis=1) + _dot_tn(k_end[:, ks].astype(BF16), vb)
        o_ref[:, vs] = _rms(o, ng_ref[...]) * _silu(g_ref[:, vs])

    @pl.when(c == n_chunks - 1)
    def _():
        sout_ref[...] = s_scr[...]


def _gla(q, k, v, g, glr, s0, w2, gb, ng, t_valid):
    bsz, t_pad, _ = q.shape
    L = GLA_CHUNK
    nc = t_pad // L
    kern = functools.partial(_gla_kernel, n_chunks=nc, t_valid=t_valid)
    tok = lambda w: pl.BlockSpec((None, L, w), lambda b, c: (b, c, 0))
    st = pl.BlockSpec((None, GLA_HEADS * GLA_DK, GLA_DV), lambda b, c: (b, 0, 0))
    return pl.pallas_call(
        kern, grid=(bsz, nc),
        in_specs=[tok(GLA_HEADS * GLA_DK), tok(GLA_HEADS * GLA_DK), tok(GLA_HEADS * GLA_DV), tok(GLA_HEADS * GLA_DV),
                  tok(LANES), st, _resident(w2.shape), _resident(gb.shape), _resident(ng.shape)],
        out_specs=[tok(GLA_HEADS * GLA_DV), st],
        out_shape=[jax.ShapeDtypeStruct((bsz, t_pad, GLA_HEADS * GLA_DV), F32),
                   jax.ShapeDtypeStruct((bsz, GLA_HEADS * GLA_DK, GLA_DV), F32)],
        scratch_shapes=[pltpu.VMEM((GLA_HEADS * GLA_DK, GLA_DV), F32)],
        compiler_params=_params(("arbitrary", "arbitrary"), 32), name="gla")(q, k, v, g, glr, s0, w2, gb, ng)


def _out_proj_kernel(x_ref, a_ref, b_ref, w_ref, o_ref, *, wa):
    o_ref[...] = (x_ref[...] + _dot(a_ref[...].astype(BF16), w_ref[:wa, :])
                  + _dot(b_ref[...].astype(BF16), w_ref[wa:, :]))


def _out_proj(x, a, b, w, tm):
    n = x.shape[0]
    wa, wb = a.shape[1], b.shape[1]
    row = lambda s: pl.BlockSpec((tm, s), lambda i: (i, 0))
    return pl.pallas_call(
        functools.partial(_out_proj_kernel, wa=wa), grid=(n // tm,),
        in_specs=[row(D_MODEL), row(wa), row(wb), _resident(w.shape)],
        out_specs=row(D_MODEL), out_shape=jax.ShapeDtypeStruct((n, D_MODEL), F32),
        compiler_params=_params(("arbitrary",), 40), name="out_proj")(x, a, b, w)


def _mlp_kernel(x_ref, g_ref, wu_ref, wd_ref, gf_ref, o_ref, *, final_norm):
    x = x_ref[...]
    xb = _rms(x, g_ref[...]).astype(BF16)
    acc = x
    for c in range(D_FF // FF_CHUNK):
        sl = slice(c * FF_CHUNK, (c + 1) * FF_CHUNK)
        h = jnp.maximum(_dot(xb, wu_ref[:, sl]), 0.0)
        acc = acc + _dot((h * h).astype(BF16), wd_ref[sl, :])
    o_ref[...] = _rms(acc, gf_ref[...]) if final_norm else acc


def _mlp(x, g, wu, wd, gf, tm, final_norm):
    n = x.shape[0]
    row = pl.BlockSpec((tm, D_MODEL), lambda i: (i, 0))
    return pl.pallas_call(
        functools.partial(_mlp_kernel, final_norm=final_norm), grid=(n // tm,),
        in_specs=[row, _resident(g.shape), _resident(wu.shape), _resident(wd.shape), _resident(gf.shape)],
        out_specs=row, out_shape=jax.ShapeDtypeStruct((n, D_MODEL), F32),
        compiler_params=_params(("arbitrary",), 48), name="mlp")(x, g, wu, wd, gf)


def _fox_gate_kernel(f_ref, bf_ref, lf_ref, c_ref, ct_ref, carry):
    @pl.when(pl.program_id(1) == 0)
    def _():
        carry[...] = jnp.zeros_like(carry)

    L = GATE_CHUNK
    lf = _log_sigmoid(f_ref[...] + bf_ref[...])
    tri = jnp.where(_tril(L), 1.0, 0.0).astype(BF16)
    cblk = carry[...] + _dot_exact_l(tri, lf)
    carry[...] = cblk[L - 1:L, :]
    lf_ref[...] = lf[:, :FOX_HEADS]
    c_ref[...] = cblk
    ct_ref[...] = cblk.T[:FOX_HEADS, :]


def _fox_gate(f, bf):
    bsz, t_pad, _ = f.shape
    L = GATE_CHUNK
    return pl.pallas_call(
        _fox_gate_kernel, grid=(bsz, t_pad // L),
        in_specs=[pl.BlockSpec((None, L, LANES), lambda b, t: (b, t, 0)), _resident(bf.shape)],
        out_specs=[pl.BlockSpec((None, L, FOX_HEADS), lambda b, t: (b, t, 0)),
                   pl.BlockSpec((None, L, LANES), lambda b, t: (b, t, 0)),
                   pl.BlockSpec((None, FOX_HEADS, L), lambda b, t: (b, 0, t))],
        out_shape=[jax.ShapeDtypeStruct((bsz, t_pad, FOX_HEADS), F32),
                   jax.ShapeDtypeStruct((bsz, t_pad, LANES), F32),
                   jax.ShapeDtypeStruct((bsz, FOX_HEADS, t_pad), F32)],
        scratch_shapes=[pltpu.VMEM((1, LANES), F32)],
        compiler_params=_params(("arbitrary", "arbitrary"), 16), name="fox_gate")(f, bf)


def _fox_prompt_kernel(q_ref, k_ref, v_ref, c_ref, ct_ref, o_ref):
    hp = pl.program_id(1)
    qi = pl.program_id(2)
    tq, tk = FOX_TQ, FOX_TK
    lane = _iota((tq, LANES), 1)
    lane_lo = lane < FOX_HEAD_DIM
    q = q_ref[...] * (FOX_HEAD_DIM ** -0.5)
    cblk = c_ref[...]
    qm, c_col, head = [], [], []
    for j in range(2):
        h = 2 * hp + j
        head.append(h)
        qm.append(jnp.where(lane_lo if j == 0 else jnp.logical_not(lane_lo), q, 0.0).astype(BF16))
        c_col.append(jnp.sum(jnp.where(lane == h, cblk, 0.0), axis=-1, keepdims=True))
    causal = _iota((tq, tk), 1) <= _iota((tq, tk), 0)

    def block(ki, carry, masked):
        ks = pl.multiple_of(ki * tk, tk)
        kb = k_ref[pl.ds(ks, tk), :].astype(BF16)
        vb = v_ref[pl.ds(ks, tk), :].astype(BF16)
        out = []
        for j in range(2):
            m, l, acc = carry[3 * j:3 * j + 3]
            c_row = ct_ref[pl.ds(head[j], 1), pl.ds(ks, tk)]
            s = _dot_nt(qm[j], kb) - c_row
            if masked:
                s = jnp.where(causal, s, -jnp.inf)
            m_new = jnp.maximum(m, jnp.max(s, axis=-1, keepdims=True) + c_col[j])
            p = jnp.exp(s - (m_new - c_col[j]))
            alpha = jnp.exp(m - m_new)
            out += [m_new, alpha * l + jnp.sum(p, axis=-1, keepdims=True), alpha * acc + _dot(p.astype(BF16), vb)]
        return tuple(out)

    init = (jnp.full((tq, 1), -jnp.inf, F32), jnp.zeros((tq, 1), F32), jnp.zeros((tq, LANES), F32)) * 2
    carry = lax.fori_loop(0, qi, lambda ki, cr: block(ki, cr, False), init)
    m0, l0, a0, m1, l1, a1 = block(qi, carry, True)
    o_ref[...] = jnp.where(lane_lo, a0 / l0, a1 / l1)


def _fox_prompt(q, k, v, c, ct):
    bsz, T, _ = q.shape
    tq = FOX_TQ
    return pl.pallas_call(
        _fox_prompt_kernel, grid=(bsz, FOX_HEADS // 2, T // tq),
        in_specs=[pl.BlockSpec((None, tq, LANES), lambda b, hp, qi: (b, qi, hp)),
                  pl.BlockSpec((None, T, LANES), lambda b, hp, qi: (b, 0, hp)),
                  pl.BlockSpec((None, T, LANES), lambda b, hp, qi: (b, 0, hp)),
                  pl.BlockSpec((None, tq, LANES), lambda b, hp, qi: (b, qi, 0)),
                  pl.BlockSpec((None, FOX_HEADS, T), lambda b, hp, qi: (b, 0, 0))],
        out_specs=pl.BlockSpec((None, tq, LANES), lambda b, hp, qi: (b, qi, hp)),
        out_shape=jax.ShapeDtypeStruct((bsz, T, FOX_HEADS * FOX_HEAD_DIM), F32),
        compiler_params=_params(("arbitrary",) * 3, 32), name="fox_prompt")(q, k, v, c, ct)


def _fox_sample_kernel(pt_ref, q_ref, kn_ref, vn_ref, cnt_ref, *refs, n_steps):
    G = PAGES_PER_STEP
    k_refs, v_refs, lf_refs = refs[:G], refs[G:2 * G], refs[2 * G:3 * G]
    o_ref, qbd, m_scr, l_scr, acc_scr, carry, cncol = refs[3 * G:]
    p = pl.program_id(1)
    R = LANES
    T = R // FOX_HEADS
    width = FOX_HEADS * FOX_HEAD_DIM
    dh_shift = FOX_HEAD_DIM.bit_length() - 1
    t_shift = T.bit_length() - 1
    row = _iota((R, LANES), 0)
    lane = _iota((R, LANES), 1)

    def expand_heads(xt):
        return jnp.concatenate([jnp.broadcast_to(xt[h:h + 1, :], (T, LANES)) for h in range(FOX_HEADS)], axis=0)

    def update(s, vb):
        m = m_scr[...]
        cq = cncol[...]
        m_new = jnp.maximum(m, jnp.max(s, axis=-1, keepdims=True) + cq)
        pr = jnp.exp(s - (m_new - cq))
        alpha = jnp.exp(m - m_new)
        m_scr[...] = m_new
        l_scr[...] = alpha * l_scr[...] + jnp.sum(pr, axis=-1, keepdims=True)
        acc_scr[...] = alpha * acc_scr[...] + _dot(pr.astype(BF16), vb)

    @pl.when(p == 0)
    def _():
        q = q_ref[...] * (FOX_HEAD_DIM ** -0.5)
        qt = jnp.concatenate([q] * FOX_HEADS, axis=0)
        own = (_iota((R, width), 1) >> dh_shift) == (_iota((R, width), 0) >> t_shift)
        qbd[...] = jnp.where(own, qt, 0.0).astype(BF16)
        m_scr[...] = jnp.full_like(m_scr, -jnp.inf)
        l_scr[...] = jnp.zeros_like(l_scr)
        acc_scr[...] = jnp.zeros_like(acc_scr)
        carry[...] = jnp.zeros_like(carry)
        cn = expand_heads(cnt_ref[...])
        t_of_row = row & (T - 1)
        cncol[...] = jnp.sum(jnp.where(lane == t_of_row, cn, 0.0), axis=-1, keepdims=True)
        pad = jnp.zeros((LANES - T, width), F32)
        kb = jnp.concatenate([kn_ref[...], pad], axis=0).astype(BF16)
        vb = jnp.concatenate([vn_ref[...], pad], axis=0).astype(BF16)
        s = _dot_nt(qbd[...], kb) - cn
        update(jnp.where(lane <= t_of_row, s, -jnp.inf), vb)

    later = jnp.where(_iota((LANES, LANES), 0) > _iota((LANES, LANES), 1), 1.0, 0.0).astype(BF16)
    for i in range(G):
        lft = lf_refs[i][...]
        suf = carry[...] + _dot_exact_r(lft, later)
        carry[...] = carry[...] + jnp.sum(lft, axis=-1, keepdims=True)
        s = _dot_nt(qbd[...], k_refs[i][...].astype(BF16)) + expand_heads(suf)
        update(s, v_refs[i][...].astype(BF16))

    @pl.when(p == n_steps - 1)
    def _():
        an = acc_scr[...] / l_scr[...]
        out = jnp.zeros((T, width), F32)
        col_head = _iota((T, width), 1) >> dh_shift
        for h in range(FOX_HEADS):
            out = jnp.where(col_head == h, an[h * T:(h + 1) * T, :], out)
        o_ref[...] = out


def _fox_sample(page_table, q, kn, vn, cnt, cache_k, cache_v, lft):
    bsz, T, width = q.shape
    n_pages = page_table.shape[1]
    G = PAGES_PER_STEP
    n_steps = n_pages // G

    def page(i):
        return lambda b, p, pt: (pt[b, n_pages - 1 - (p * G + i)], 0, 0)

    per_b = lambda s: pl.BlockSpec((None,) + s, lambda b, p, pt: (b, 0, 0))
    in_specs = ([per_b((T, width)), per_b((T, width)), per_b((T, width)), per_b((FOX_HEADS, LANES))]
                + [pl.BlockSpec((None, PAGE_SIZE, width), page(i)) for i in range(G)]
                + [pl.BlockSpec((None, PAGE_SIZE, width), page(i)) for i in range(G)]
                + [pl.BlockSpec((None, FOX_HEADS, PAGE_SIZE), page(i)) for i in range(G)])
    grid_spec = pltpu.PrefetchScalarGridSpec(
        num_scalar_prefetch=1, grid=(bsz, n_steps), in_specs=in_specs, out_specs=per_b((T, width)),
        scratch_shapes=[pltpu.VMEM((LANES, width), BF16), pltpu.VMEM((LANES, 1), F32), pltpu.VMEM((LANES, 1), F32),
                        pltpu.VMEM((LANES, width), F32), pltpu.VMEM((FOX_HEADS, 1), F32), pltpu.VMEM((LANES, 1), F32)])
    return pl.pallas_call(
        functools.partial(_fox_sample_kernel, n_steps=n_steps), grid_spec=grid_spec,
        out_shape=jax.ShapeDtypeStruct((bsz, T, width), F32),
        compiler_params=_params(("arbitrary", "arbitrary"), 40), name="fox_sample")(
            page_table, q, kn, vn, cnt, *([cache_k] * G), *([cache_v] * G), *([lft] * G))


def _conf_kernel(u_ref, st0_ref, cw_ref, cb_ref, lg_ref, lb_ref, c_ref, stout_ref, buf, *, chunk, n_chunks):
    t = pl.program_id(1)
    L = chunk
    P0 = CONF_PAD - (CONF_WIDTH - 1)

    @pl.when(t == 0)
    def _():
        buf[P0:CONF_PAD, :] = st0_ref[...]

    u = u_ref[...]
    buf[CONF_PAD:CONF_PAD + L, :] = u[:, :CONF_CH] * _sigmoid(u[:, CONF_CH:])
    cw = cw_ref[...]
    acc = cb_ref[...] + buf[P0:P0 + L, :] * cw[0:1, :]
    for j in range(1, CONF_WIDTH):
        acc = acc + buf[P0 + j:P0 + j + L, :] * cw[j:j + 1, :]
    xc = acc - jnp.mean(acc, axis=-1, keepdims=True)
    var = jnp.mean(xc * xc, axis=-1, keepdims=True)
    c_ref[...] = _silu(xc * lax.rsqrt(var + EPS) * lg_ref[...] + lb_ref[...])

    @pl.when(t == n_chunks - 1)
    def _():
        stout_ref[...] = buf[P0 + L:CONF_PAD + L, :]

    tail = buf[L:L + CONF_PAD, :]
    buf[0:CONF_PAD, :] = tail


def _conf(u, st0, cw, cb, lg, lb, chunk):
    bsz, T, _ = u.shape
    nc = T // chunk
    st = pl.BlockSpec((None, CONF_WIDTH - 1, CONF_CH), lambda b, t: (b, 0, 0))
    return pl.pallas_call(
        functools.partial(_conf_kernel, chunk=chunk, n_chunks=nc), grid=(bsz, nc),
        in_specs=[pl.BlockSpec((None, chunk, 2 * CONF_CH), lambda b, t: (b, t, 0)), st, _resident(cw.shape),
                  _resident(cb.shape), _resident(lg.shape), _resident(lb.shape)],
        out_specs=[pl.BlockSpec((None, chunk, CONF_CH), lambda b, t: (b, t, 0)), st],
        out_shape=[jax.ShapeDtypeStruct((bsz, T, CONF_CH), F32),
                   jax.ShapeDtypeStruct((bsz, CONF_WIDTH - 1, CONF_CH), F32)],
        scratch_shapes=[pltpu.VMEM((CONF_PAD + chunk, CONF_CH), F32)],
        compiler_params=_params(("arbitrary", "arbitrary"), 16), name="conf_conv")(u, st0, cw, cb, lg, lb)


def _pad_cols(w, width):
    return jnp.pad(w, ((0, 0), (0, width - w.shape[1])))


def _split_cols(w, sizes):
    out, off = [], 0
    for s in sizes:
        out.append(w[:, off:off + s])
        off += s
    return out


def _pad_t(a, bsz, t, t_pad):
    a = a.reshape(bsz, t, a.shape[-1])
    return a if t_pad == t else jnp.pad(a, ((0, 0), (0, t_pad - t), (0, 0)))


def _ceil_to(x, m):
    return -(-x // m) * m


def kernel(x_prompt, x_sample, cache_fox_k, cache_fox_v, cache_fox_logf, page_table, state_ssm, state_ssm_conv, state_gla, state_conf_conv, g_mix, g_mlp, g_final, w_in_ab, ssm_conv_w, ssm_conv_b, ssm_dt_bias, ssm_a_log, ssm_d, ssm_norm_g, gla_gate_w2, gla_gate_b, gla_norm_g, w_out_ab, w_in_cd, fox_b_f, conf_conv_w, conf_conv_b, conf_ln_g, conf_ln_b, w_out_cd, w_mlp_up, w_mlp_down):
    row = lambda v: v.reshape(1, -1)
    wz, wxbc, wdt, wq, wk, wv, wg, wlr = _split_cols(w_in_ab[0], AB_SPLIT)
    w_ab = jnp.concatenate([wz, wxbc, wq, wk, wv, wg, _pad_cols(wdt, LANES), _pad_cols(wlr, LANES)], axis=1).astype(BF16)
    cq, ck, cv, cf, cu = _split_cols(w_in_cd[0], CD_SPLIT)
    w_cd = jnp.concatenate([cq, ck, cv, cu, _pad_cols(cf, LANES)], axis=1).astype(BF16)
    w_oab = w_out_ab[0].astype(BF16)
    w_ocd = w_out_cd[0].astype(BF16)
    w_up = w_mlp_up.astype(BF16)
    w_dn = w_mlp_down.astype(BF16)
    dtb = _pad_cols(row(ssm_dt_bias[0]), LANES)
    alog = _pad_cols(row(ssm_a_log[0]), LANES)
    dsk = row(jnp.repeat(ssm_d[0], SSM_HEAD_DIM))
    e16 = (jnp.arange(LANES)[:, None] == jnp.arange(SSM_INNER)[None, :] // SSM_HEAD_DIM).astype(BF16)
    w2 = jnp.pad(gla_gate_w2[0], ((0, LANES - GLA_RANK), (0, 0))).astype(BF16)
    bfp = _pad_cols(row(fox_b_f[0]), LANES)
    width = FOX_HEADS * FOX_HEAD_DIM
    n_pool = cache_fox_k.shape[1]
    cache_k = cache_fox_k[0].reshape(n_pool, PAGE_SIZE, width)
    cache_v = cache_fox_v[0].reshape(n_pool, PAGE_SIZE, width)
    lft = jnp.swapaxes(cache_fox_logf[0], 1, 2)

    def trunk(x3, sample):
        bsz, T, _ = x3.shape
        n = bsz * T
        tm = min(n, 256)
        tm_big = min(n, 512)
        x = x3.reshape(n, D_MODEL)
        if sample:
            h0 = state_ssm[0].reshape(bsz, SSM_INNER, SSM_STATE)
            conv0, s0, conf0 = state_ssm_conv[0], state_gla[0].reshape(bsz, GLA_HEADS * GLA_DK, GLA_DV), state_conf_conv[0]
        else:
            h0 = jnp.zeros((bsz, SSM_INNER, SSM_STATE), F32)
            conv0 = jnp.zeros((bsz, SSM_CONV - 1, SSM_CONV_CH), F32)
            s0 = jnp.zeros((bsz, GLA_HEADS * GLA_DK, GLA_DV), F32)
            conf0 = jnp.zeros((bsz, CONF_WIDTH - 1, CONF_CH), F32)

        z, xbc, q, k, v, g, dtp, glr = _norm_proj(x, row(g_mix[0]), w_ab, AB_SECTIONS, tm)
        ta = _ceil_to(T, SSD_CHUNK)
        y, h_new, conv_new = _ssd(_pad_t(z, bsz, T, ta), _pad_t(xbc, bsz, T, ta), _pad_t(dtp, bsz, T, ta), conv0, h0,
                                  ssm_conv_w[0], row(ssm_conv_b[0]), dtb, alog, dsk, row(ssm_norm_g[0]), e16, T)
        tb = _ceil_to(T, GLA_CHUNK)
        o, s_new = _gla(_pad_t(q, bsz, T, tb), _pad_t(k, bsz, T, tb), _pad_t(v, bsz, T, tb), _pad_t(g, bsz, T, tb),
                        _pad_t(glr, bsz, T, tb), s0, w2, row(gla_gate_b[0]), row(gla_norm_g[0]), T)
        y = y[:, :T].reshape(n, SSM_INNER)
        o = o[:, :T].reshape(n, GLA_HEADS * GLA_DV)
        x = _out_proj(x, y, o, w_oab, tm_big)
        x = _mlp(x, row(g_mlp[0]), w_up[0], w_dn[0], row(g_final), tm_big, False)

        q, k, v, u, f = _norm_proj(x, row(g_mix[1]), w_cd, CD_SECTIONS, tm)
        tg = _ceil_to(T, GATE_CHUNK)
        lf, c, ct = _fox_gate(_pad_t(f, bsz, T, tg), bfp)
        q3, k3, v3 = (a.reshape(bsz, T, width) for a in (q, k, v))
        if sample:
            att = _fox_sample(page_table, q3, k3, v3, ct, cache_k, cache_v, lft)
        else:
            att = _fox_prompt(q3, k3, v3, c, ct)
        cmod, conf_new = _conf(u.reshape(bsz, T, 2 * CONF_CH), conf0, conf_conv_w[0], row(conf_conv_b[0]),
                               row(conf_ln_g[0]), row(conf_ln_b[0]), min(T, CONF_CHUNK))
        x = _out_proj(x, att.reshape(n, width), cmod.reshape(n, CONF_CH), w_ocd, tm_big)
        x = _mlp(x, row(g_mlp[1]), w_up[1], w_dn[1], row(g_final), tm_big, True)
        return (x.reshape(bsz, T, D_MODEL), h_new.reshape(1, bsz, SSM_HEADS, SSM_HEAD_DIM, SSM_STATE), conv_new[None],
                s_new.reshape(1, bsz, GLA_HEADS, GLA_DK, GLA_DV), k.reshape(1, bsz, T, FOX_HEADS, FOX_HEAD_DIM),
                v.reshape(1, bsz, T, FOX_HEADS, FOX_HEAD_DIM), lf[None, :, :T], conf_new[None])

    yp, *rest_p = trunk(x_prompt, False)
    ys, *rest_s = trunk(x_sample, True)
    return (yp, ys, *rest_p, *rest_s)
```

```python
import functools

import jax
import jax.numpy as jnp
from jax import lax
from jax.experimental import pallas as pl
from jax.experimental.pallas import tpu as pltpu

F32 = jnp.float32
BF16 = jnp.bfloat16
EPS = 1e-6
LOG2E = 1.4426950408889634

LANES = 128
MIB = 1024 * 1024

D_MODEL = 1024
D_FF = 4 * D_MODEL
SSM_HEADS = 16
SSM_HEAD_DIM = 64
SSM_INNER = SSM_HEADS * SSM_HEAD_DIM
SSM_GROUPS = 2
SSM_STATE = 128
SSM_CONV = 4
SSM_CONV_CH = SSM_INNER + 2 * SSM_GROUPS * SSM_STATE
GLA_HEADS = 4
GLA_DK = 128
GLA_DV = 256
GLA_RANK = 16
GLA_GATE_NORM = 16.0
FOX_HEADS = 16
FOX_HEAD_DIM = 64
CONF_CH = 512
CONF_WIDTH = 31
PAGE_SIZE = 128

AB_SPLIT = (SSM_INNER, SSM_CONV_CH, SSM_HEADS, GLA_HEADS * GLA_DK, GLA_HEADS * GLA_DK,
            GLA_HEADS * GLA_DV, GLA_HEADS * GLA_DV, GLA_RANK)
CD_SPLIT = (FOX_HEADS * FOX_HEAD_DIM,) * 3 + (FOX_HEADS, 2 * CONF_CH)
AB_SECTIONS = (SSM_INNER, SSM_CONV_CH, GLA_HEADS * GLA_DK, GLA_HEADS * GLA_DK, GLA_HEADS * GLA_DV,
               GLA_HEADS * GLA_DV, LANES, LANES)
CD_SECTIONS = (1024, 1024, 1024, 2 * CONF_CH, LANES)

SSD_CHUNK = 128
GLA_CHUNK = 64
GATE_CHUNK = 128
FOX_TQ = 256
FOX_TK = 256
CONF_CHUNK = 256
CONF_PAD = 32
SSD_PAD = 8
FF_CHUNK = 1024
PAGES_PER_STEP = 8


def _dot(a, b):
    return jnp.dot(a, b, preferred_element_type=F32)


def _dot_nt(a, b):
    return lax.dot_general(a, b, (((1,), (1,)), ((), ())), preferred_element_type=F32)


def _dot_tn(a, b):
    return lax.dot_general(a, b, (((0,), (0,)), ((), ())), preferred_element_type=F32)


def _split3(x):
    hi = x.astype(BF16)
    r1 = x - hi.astype(F32)
    mid = r1.astype(BF16)
    lo = (r1 - mid.astype(F32)).astype(BF16)
    return hi, mid, lo


def _dot_exact_l(m, x):
    hi, mid, lo = _split3(x)
    return _dot(m, lo) + _dot(m, mid) + _dot(m, hi)


def _dot_exact_r(x, m):
    hi, mid, lo = _split3(x)
    return _dot(lo, m) + _dot(mid, m) + _dot(hi, m)


def _sigmoid(x):
    return 1.0 / (1.0 + jnp.exp(-x))


def _silu(x):
    return x * _sigmoid(x)


def _softplus(x):
    return jnp.maximum(x, 0.0) + jnp.log1p(jnp.exp(-jnp.abs(x)))


def _log_sigmoid(x):
    return jnp.minimum(x, 0.0) - jnp.log1p(jnp.exp(-jnp.abs(x)))


def _rms(x, g):
    return x * lax.rsqrt(jnp.mean(x * x, axis=-1, keepdims=True) + EPS) * g


def _iota(shape, dim):
    return lax.broadcasted_iota(jnp.int32, shape, dim)


def _tril(n):
    return _iota((n, n), 1) <= _iota((n, n), 0)


def _params(semantics, vmem_mib):
    return pltpu.CompilerParams(dimension_semantics=semantics, vmem_limit_bytes=vmem_mib * MIB)


def _resident(shape):
    nd = len(shape)
    return pl.BlockSpec(shape, lambda *_: (0,) * nd, pipeline_mode=pl.Buffered(1))


def _norm_proj_kernel(x_ref, g_ref, w_ref, *out_refs, sections):
    xb = _rms(x_ref[...], g_ref[...]).astype(BF16)
    outs = iter(out_refs)
    for off, width, mode in sections:
        y = _dot(xb, w_ref[:, off:off + width])
        if "n" in mode:
            next(outs)[...] = y
        if "t" in mode:
            next(outs)[...] = y.T


def _norm_proj(x, g, w, widths, tm, modes=None, seq=None):
    n = x.shape[0]
    modes = modes or ("n",) * len(widths)
    offs = [sum(widths[:i]) for i in range(len(widths))]
    kern = functools.partial(_norm_proj_kernel, sections=tuple(zip(offs, widths, modes)))
    wtot = w.shape[1]
    out_specs, out_shape = [], []
    for s, mode in zip(widths, modes):
        if "n" in mode:
            out_specs.append(pl.BlockSpec((tm, s), lambda i: (i, 0)))
            out_shape.append(jax.ShapeDtypeStruct((n, s), F32))
        if "t" in mode:
            per_seq = seq // tm
            out_specs.append(pl.BlockSpec((None, s, tm), lambda i: (i // per_seq, 0, i % per_seq)))
            out_shape.append(jax.ShapeDtypeStruct((n // seq, s, seq), F32))
    n_out = sum(s * len(mode) for s, mode in zip(widths, modes))
    vmem = (2 * tm * D_MODEL * 4 + D_MODEL * wtot * 2 + 3 * tm * n_out * 4) // MIB + 4
    return pl.pallas_call(
        kern, grid=(n // tm,),
        in_specs=[pl.BlockSpec((tm, D_MODEL), lambda i: (i, 0)), _resident((1, D_MODEL)), _resident(w.shape)],
        out_specs=out_specs, out_shape=out_shape,
        compiler_params=_params(("arbitrary",), vmem), name="norm_proj")(x, g, w)


def _ssd_kernel(z_ref, xbc_ref, dt_ref, conv0_ref, h0_ref, cw_ref, cb_ref, dtb_ref, alog_ref, dsk_ref, ng_ref,
                e16_ref, y_ref, hout_ref, convout_ref, xbuf, h_scr, y_scr, *, n_chunks, t_valid):
    c = pl.program_id(1)
    L = SSD_CHUNK
    P0 = SSD_PAD - (SSM_CONV - 1)
    last_valid = t_valid - (n_chunks - 1) * L

    @pl.when(c == 0)
    def _():
        xbuf[P0:SSD_PAD, :] = conv0_ref[...]
        h_scr[...] = h0_ref[...]

    xbuf[SSD_PAD:SSD_PAD + L, :] = xbc_ref[...]
    cw = cw_ref[...]
    conv = cb_ref[...] + xbuf[P0:P0 + L, :] * cw[0:1, :]
    for j in range(1, SSM_CONV):
        conv = conv + xbuf[P0 + j:P0 + j + L, :] * cw[j:j + 1, :]
    act = _silu(conv)
    xs = act[:, :SSM_INNER]
    bm_b = act[:, SSM_INNER:SSM_INNER + SSM_GROUPS * SSM_STATE].astype(BF16)
    cm_b = act[:, SSM_INNER + SSM_GROUPS * SSM_STATE:].astype(BF16)

    row = _iota((L, LANES), 0)
    lane = _iota((L, LANES), 1)
    live = (lane < SSM_HEADS) & (c * L + row < t_valid)
    dt = jnp.where(live, _softplus(dt_ref[...] + dtb_ref[...]), 0.0)
    a = -jnp.exp(alog_ref[...]) * dt
    causal = _tril(L)
    tri = jnp.where(causal, 1.0, 0.0).astype(BF16)
    ac = _dot_exact_l(tri, a)
    e16 = e16_ref[...]
    ac_x = _dot_exact_r(ac, e16)
    dt_x = _dot_exact_r(dt, e16)
    ac_t = ac.T
    ac_last = ac[L - 1:L, :]
    ac_last_x = ac_x[L - 1:L, :]
    to_end_x = jnp.exp(ac_last_x - ac_x)
    eac_x = jnp.exp(ac_x)
    xdt = xs * dt_x
    xdt_b = xdt.astype(BF16)
    xend_b = (xdt * to_end_x).astype(BF16)
    lane_lo = lane < SSM_HEAD_DIM
    row_lo = _iota((LANES, LANES), 0) < SSM_HEAD_DIM
    heads_per_group = SSM_HEADS // SSM_GROUPS
    for g in range(SSM_GROUPS):
        bg = bm_b[:, g * SSM_STATE:(g + 1) * SSM_STATE]
        cg = cm_b[:, g * SSM_STATE:(g + 1) * SSM_STATE]
        cb = _dot_nt(cg, bg)
        for p in range(g * heads_per_group // 2, (g + 1) * heads_per_group // 2):
            sl = slice(p * LANES, (p + 1) * LANES)
            ys = []
            for j in range(2):
                h = 2 * p + j
                seg = ac[:, h:h + 1] - ac_t[h:h + 1, :]
                m = (cb * jnp.exp(jnp.where(causal, seg, -jnp.inf))).astype(BF16)
                ys.append(_dot(m, xdt_b[:, sl]))
            y_diag = jnp.where(lane_lo, ys[0], ys[1])
            s_old = h_scr[sl, :]
            y_off = _dot_nt(cg, s_old.astype(BF16)) * eac_x[:, sl]
            dec = jnp.exp(jnp.where(row_lo, ac_last[:, 2 * p:2 * p + 1], ac_last[:, 2 * p + 1:2 * p + 2]))
            h_scr[sl, :] = s_old * dec + _dot_tn(xend_b[:, sl], bg)
            y_scr[:, sl] = y_diag + y_off + dsk_ref[:, sl] * xs[:, sl]

    y_ref[...] = _rms(y_scr[...] * _silu(z_ref[...]), ng_ref[...])

    @pl.when(c == n_chunks - 1)
    def _():
        hout_ref[...] = h_scr[...]
        convout_ref[...] = xbuf[P0 + last_valid:SSD_PAD + last_valid, :]

    tail = xbuf[P0 + L:SSD_PAD + L, :]
    xbuf[P0:SSD_PAD, :] = tail


def _ssd(z, xbc, dtp, conv0, h0, cw, cb, dtb, alog, dsk, ng, e16, t_valid):
    bsz, t_pad, _ = z.shape
    L = SSD_CHUNK
    nc = t_pad // L
    kern = functools.partial(_ssd_kernel, n_chunks=nc, t_valid=t_valid)
    tok = lambda w: pl.BlockSpec((None, L, w), lambda b, c: (b, c, 0))
    per_b = lambda s: pl.BlockSpec((None,) + s, lambda b, c: (b, 0, 0))
    return pl.pallas_call(
        kern, grid=(bsz, nc),
        in_specs=[tok(SSM_INNER), tok(SSM_CONV_CH), tok(LANES), per_b((SSM_CONV - 1, SSM_CONV_CH)),
                  per_b((SSM_INNER, SSM_STATE)), _resident(cw.shape), _resident(cb.shape), _resident(dtb.shape),
                  _resident(alog.shape), _resident(dsk.shape), _resident(ng.shape), _resident(e16.shape)],
        out_specs=[tok(SSM_INNER), per_b((SSM_INNER, SSM_STATE)), per_b((SSM_CONV - 1, SSM_CONV_CH))],
        out_shape=[jax.ShapeDtypeStruct((bsz, t_pad, SSM_INNER), F32),
                   jax.ShapeDtypeStruct((bsz, SSM_INNER, SSM_STATE), F32),
                   jax.ShapeDtypeStruct((bsz, SSM_CONV - 1, SSM_CONV_CH), F32)],
        scratch_shapes=[pltpu.VMEM((SSD_PAD + L, SSM_CONV_CH), F32), pltpu.VMEM((SSM_INNER, SSM_STATE), F32),
                        pltpu.VMEM((L, SSM_INNER), F32)],
        compiler_params=_params(("arbitrary", "arbitrary"), 40), name="ssd")(
            z, xbc, dtp, conv0, h0, cw, cb, dtb, alog, dsk, ng, e16)


def _gla_kernel(q_ref, k_ref, v_ref, g_ref, glr_ref, s0_ref, w2_ref, gb_ref, ng_ref, o_ref, sout_ref, s_scr,
                *, n_chunks, t_valid):
    c = pl.program_id(1)
    L = GLA_CHUNK
    width = GLA_HEADS * GLA_DK

    @pl.when(c == 0)
    def _():
        s_scr[...] = s0_ref[...]

    x = _dot(glr_ref[...].astype(BF16), w2_ref[...]) + gb_ref[...]
    logf = _log_sigmoid(x) * (1.0 / GLA_GATE_NORM)
    logf = jnp.where(c * L + _iota((L, width), 0) < t_valid, logf, 0.0)
    causal = _tril(L)
    tri = jnp.where(causal, 1.0, 0.0).astype(BF16)
    bcum = _dot_exact_l(tri, logf)
    bl = bcum[L - 1:L, :]
    q_dec = q_ref[...] * (GLA_DK ** -0.5) * jnp.exp(bcum)
    k = k_ref[...]
    k_inv = k * jnp.exp(-bcum)
    k_end = k * jnp.exp(bl - bcum)
    dec_row = jnp.exp(bl)
    for h in range(GLA_HEADS):
        ks = slice(h * GLA_DK, (h + 1) * GLA_DK)
        vs = slice(h * GLA_DV, (h + 1) * GLA_DV)
        qd = q_dec[:, ks].astype(BF16)
        vb = v_ref[:, vs].astype(BF16)
        att = jnp.where(causal, _dot_nt(qd, k_inv[:, ks].astype(BF16)), 0.0)
        s_old = s_scr[ks, :]
        o = _dot(att.astype(BF16), vb) + _dot(qd, s_old.astype(BF16))
        dcol = jnp.broadcast_to(dec_row[:, ks], (GLA_DK, GLA_DK)).T
        s_scr[ks, :] = s_old * jnp.concatenate([dcol, dcol], axis=1) + _dot_tn(k_end[:, ks].astype(BF16), vb)
        o_ref[:, vs] = _rms(o, ng_ref[...]) * _silu(g_ref[:, vs])

    @pl.when(c == n_chunks - 1)
    def _():
        sout_ref[...] = s_scr[...]


def _gla(q, k, v, g, glr, s0, w2, gb, ng, t_valid):
    bsz, t_pad, _ = q.shape
    L = GLA_CHUNK
    nc = t_pad // L
    kern = functools.partial(_gla_kernel, n_chunks=nc, t_valid=t_valid)
    tok = lambda w: pl.BlockSpec((None, L, w), lambda b, c: (b, c, 0))
    st = pl.BlockSpec((None, GLA_HEADS * GLA_DK, GLA_DV), lambda b, c: (b, 0, 0))
    return pl.pallas_call(
        kern, grid=(bsz, nc),
        in_specs=[tok(GLA_HEADS * GLA_DK), tok(GLA_HEADS * GLA_DK), tok(GLA_HEADS * GLA_DV), tok(GLA_HEADS * GLA_DV),
                  tok(LANES), st, _resident(w2.shape), _resident(gb.shape), _resident(ng.shape)],
        out_specs=[tok(GLA_HEADS * GLA_DV), st],
        out_shape=[jax.ShapeDtypeStruct((bsz, t_pad, GLA_HEADS * GLA_DV), F32),
                   jax.ShapeDtypeStruct((bsz, GLA_HEADS * GLA_DK, GLA_DV), F32)],
        scratch_shapes=[pltpu.VMEM((GLA_HEADS * GLA_DK, GLA_DV), F32)],
        compiler_params=_params(("arbitrary", "arbitrary"), 32), name="gla")(q, k, v, g, glr, s0, w2, gb, ng)


def _out_proj_kernel(x_ref, a_ref, b_ref, w_ref, o_ref, *, wa):
    o_ref[...] = (x_ref[...] + _dot(a_ref[...].astype(BF16), w_ref[:wa, :])
                  + _dot(b_ref[...].astype(BF16), w_ref[wa:, :]))


def _out_proj(x, a, b, w, tm):
    n = x.shape[0]
    wa, wb = a.shape[1], b.shape[1]
    row = lambda s: pl.BlockSpec((tm, s), lambda i: (i, 0))
    return pl.pallas_call(
        functools.partial(_out_proj_kernel, wa=wa), grid=(n // tm,),
        in_specs=[row(D_MODEL), row(wa), row(wb), _resident(w.shape)],
        out_specs=row(D_MODEL), out_shape=jax.ShapeDtypeStruct((n, D_MODEL), F32),
        compiler_params=_params(("arbitrary",), 40), name="out_proj")(x, a, b, w)


def _mlp_kernel(x_ref, g_ref, wu_ref, wd_ref, gf_ref, o_ref, *, final_norm):
    x = x_ref[...]
    xb = _rms(x, g_ref[...]).astype(BF16)
    acc = x
    for c in range(D_FF // FF_CHUNK):
        sl = slice(c * FF_CHUNK, (c + 1) * FF_CHUNK)
        h = jnp.maximum(_dot(xb, wu_ref[:, sl]), 0.0)
        acc = acc + _dot((h * h).astype(BF16), wd_ref[sl, :])
    o_ref[...] = _rms(acc, gf_ref[...]) if final_norm else acc


def _mlp(x, g, wu, wd, gf, tm, final_norm):
    n = x.shape[0]
    row = pl.BlockSpec((tm, D_MODEL), lambda i: (i, 0))
    return pl.pallas_call(
        functools.partial(_mlp_kernel, final_norm=final_norm), grid=(n // tm,),
        in_specs=[row, _resident(g.shape), _resident(wu.shape), _resident(wd.shape), _resident(gf.shape)],
        out_specs=row, out_shape=jax.ShapeDtypeStruct((n, D_MODEL), F32),
        compiler_params=_params(("arbitrary",), 48), name="mlp")(x, g, wu, wd, gf)


def _fox_gate_kernel(f_ref, bf_ref, lft_ref, c_ref, ct_ref, carry):
    @pl.when(pl.program_id(1) == 0)
    def _():
        carry[...] = jnp.zeros_like(carry)

    L = GATE_CHUNK
    lf = _log_sigmoid(f_ref[...] + bf_ref[...])
    tri = jnp.where(_tril(L), 1.0, 0.0).astype(BF16)
    cblk = carry[...] + _dot_exact_l(tri, lf)
    carry[...] = cblk[L - 1:L, :]
    lft_ref[...] = lf.T[:FOX_HEADS, :]
    c_ref[...] = cblk
    ct_ref[...] = cblk.T[:FOX_HEADS, :]


def _fox_gate(f, bf):
    bsz, t_pad, _ = f.shape
    L = GATE_CHUNK
    return pl.pallas_call(
        _fox_gate_kernel, grid=(bsz, t_pad // L),
        in_specs=[pl.BlockSpec((None, L, LANES), lambda b, t: (b, t, 0)), _resident(bf.shape)],
        out_specs=[pl.BlockSpec((None, FOX_HEADS, L), lambda b, t: (b, 0, t)),
                   pl.BlockSpec((None, L, LANES), lambda b, t: (b, t, 0)),
                   pl.BlockSpec((None, FOX_HEADS, L), lambda b, t: (b, 0, t))],
        out_shape=[jax.ShapeDtypeStruct((bsz, FOX_HEADS, t_pad), F32),
                   jax.ShapeDtypeStruct((bsz, t_pad, LANES), F32),
                   jax.ShapeDtypeStruct((bsz, FOX_HEADS, t_pad), F32)],
        scratch_shapes=[pltpu.VMEM((1, LANES), F32)],
        compiler_params=_params(("arbitrary", "arbitrary"), 16), name="fox_gate")(f, bf)


def _fox_prompt_kernel(q_ref, k_ref, vt_ref, c_ref, ct_ref, o_ref, kb_scr, vtb_scr, ck_scr, s_scr, p_scr, *,
                       n_q_blocks):
    hp = pl.program_id(1)
    qi = pl.program_id(2)
    tq, tk = FOX_TQ, FOX_TK
    dh = FOX_HEAD_DIM
    heads = [2 * hp, 2 * hp + 1]

    @pl.when(qi == 0)
    def _():
        kb_scr[...] = k_ref[...].astype(BF16)
        vtb_scr[...] = vt_ref[...].astype(BF16)
        for j in range(2):
            pick = jnp.where(_iota((LANES, LANES), 0) == heads[j], 1.0, 0.0).astype(BF16)
            ck_scr[j] = _dot_exact_r(c_ref[...], pick) * LOG2E

    lane = _iota((tq, LANES), 1)
    q = q_ref[...] * (dh ** -0.5 * LOG2E)
    qs = pl.multiple_of(qi * tq, tq)
    qm = [jnp.where(lane < dh, q, 0.0).astype(BF16), jnp.where(lane >= dh, q, 0.0).astype(BF16)]
    cq = [ct_ref[pl.ds(heads[j], 1), pl.ds(qs, tq)] * LOG2E for j in range(2)]
    causal = _iota((tk, tq), 0) <= _iota((tk, tq), 1)

    def fold8(x, op):
        parts = [x[r * 8:(r + 1) * 8, :] for r in range(x.shape[0] // 8)]
        while len(parts) > 1:
            parts = [op(parts[i], parts[i + 1]) for i in range(0, len(parts), 2)]
        return parts[0]

    def attend(n_blocks):
        n_k = n_blocks * tk
        outs = []
        for j in range(2):
            s_all = _dot_nt(kb_scr[0:n_k, :], qm[j])
            mx = jnp.full((8, tq), -jnp.inf, F32)
            for ki in range(n_blocks):
                rows = slice(ki * tk, (ki + 1) * tk)
                s = s_all[rows, :] - jnp.concatenate([ck_scr[j, rows, :]] * (tq // LANES), axis=1)
                if ki == n_blocks - 1:
                    s = jnp.where(causal, s, -jnp.inf)
                s_scr[j, rows, :] = s
                mx = jnp.maximum(mx, fold8(s, jnp.maximum))
            m_new = jnp.max(mx, axis=0, keepdims=True) + cq[j]
            shift = m_new - cq[j]
            lsum = jnp.zeros((8, tq), F32)
            for ki in range(n_blocks):
                rows = slice(ki * tk, (ki + 1) * tk)
                p = jnp.exp2(s_scr[j, rows, :] - shift)
                lsum = lsum + fold8(p, jnp.add)
                p_scr[j, rows, :] = p.astype(BF16)
            acc = _dot(vtb_scr[j * dh:(j + 1) * dh, 0:n_k], p_scr[j, 0:n_k, :])
            outs.append(acc / jnp.sum(lsum, axis=0, keepdims=True))
        o_ref[...] = jnp.concatenate(outs, axis=0).T

    for v in range(n_q_blocks):
        pl.when(qi == v)(functools.partial(attend, v + 1))


def _fox_prompt(q, k, vt, c, ct):
    bsz, T, _ = q.shape
    tq = FOX_TQ
    return pl.pallas_call(
        functools.partial(_fox_prompt_kernel, n_q_blocks=T // tq), grid=(bsz, FOX_HEADS // 2, T // tq),
        in_specs=[pl.BlockSpec((None, tq, LANES), lambda b, hp, qi: (b, qi, hp)),
                  pl.BlockSpec((None, T, LANES), lambda b, hp, qi: (b, 0, hp)),
                  pl.BlockSpec((None, LANES, T), lambda b, hp, qi: (b, hp, 0)),
                  pl.BlockSpec((None, T, LANES), lambda b, hp, qi: (b, 0, 0)),
                  pl.BlockSpec((None, FOX_HEADS, T), lambda b, hp, qi: (b, 0, 0))],
        out_specs=pl.BlockSpec((None, tq, LANES), lambda b, hp, qi: (b, qi, hp)),
        out_shape=jax.ShapeDtypeStruct((bsz, T, FOX_HEADS * FOX_HEAD_DIM), F32),
        scratch_shapes=[pltpu.VMEM((T, LANES), BF16), pltpu.VMEM((LANES, T), BF16), pltpu.VMEM((2, T, LANES), F32),
                        pltpu.VMEM((2, T, tq), F32), pltpu.VMEM((2, T, tq), BF16)],
        compiler_params=_params(("arbitrary",) * 3, 32), name="fox_prompt")(q, k, vt, c, ct)


def _fox_sample_kernel(pt_ref, q_ref, kn_ref, vn_ref, cnt_ref, *refs, n_steps):
    G = PAGES_PER_STEP
    k_refs, v_refs, lf_refs = refs[:G], refs[G:2 * G], refs[2 * G:3 * G]
    o_ref, qbd, kt_scr, vt_scr, m_scr, l_scr, acc_scr, carry, cncol = refs[3 * G:]
    p = pl.program_id(1)
    R = LANES
    T = R // FOX_HEADS
    width = FOX_HEADS * FOX_HEAD_DIM
    dh_shift = FOX_HEAD_DIM.bit_length() - 1
    t_shift = T.bit_length() - 1
    row = _iota((R, LANES), 0)
    lane = _iota((R, LANES), 1)

    def expand_heads(xt):
        return jnp.concatenate([jnp.broadcast_to(xt[h:h + 1, :], (T, xt.shape[1])) for h in range(FOX_HEADS)], axis=0)

    def update(s, vb):
        m = m_scr[...]
        cq = cncol[...]
        m_new = jnp.maximum(m, jnp.max(s, axis=-1, keepdims=True) + cq)
        pr = jnp.exp(s - (m_new - cq))
        alpha = jnp.exp(m - m_new)
        m_scr[...] = m_new
        l_scr[...] = alpha * l_scr[...] + jnp.sum(pr, axis=-1, keepdims=True)
        acc_scr[...] = alpha * acc_scr[...] + _dot(pr.astype(BF16), vb)

    @pl.when(p == 0)
    def _():
        q = q_ref[...] * (FOX_HEAD_DIM ** -0.5)
        qt = jnp.concatenate([q] * FOX_HEADS, axis=0)
        own = (_iota((R, width), 1) >> dh_shift) == (_iota((R, width), 0) >> t_shift)
        qbd[...] = jnp.where(own, qt, 0.0).astype(BF16)
        m_scr[...] = jnp.full_like(m_scr, -jnp.inf)
        l_scr[...] = jnp.zeros_like(l_scr)
        acc_scr[...] = jnp.zeros_like(acc_scr)
        carry[...] = jnp.zeros_like(carry)
        cn = expand_heads(cnt_ref[...])
        t_of_row = row & (T - 1)
        cncol[...] = jnp.sum(jnp.where(lane == t_of_row, cn, 0.0), axis=-1, keepdims=True)
        pad = jnp.zeros((LANES - T, width), F32)
        kb = jnp.concatenate([kn_ref[...], pad], axis=0).astype(BF16)
        vb = jnp.concatenate([vn_ref[...], pad], axis=0).astype(BF16)
        s = _dot_nt(qbd[...], kb) - cn
        update(jnp.where(lane <= t_of_row, s, -jnp.inf), vb)

    for i in range(G):
        kt_scr[:, i * PAGE_SIZE:(i + 1) * PAGE_SIZE] = k_refs[i][...].astype(BF16)
        vt_scr[:, i * PAGE_SIZE:(i + 1) * PAGE_SIZE] = v_refs[i][...].astype(BF16)
    later = jnp.where(_iota((LANES, LANES), 0) > _iota((LANES, LANES), 1), 1.0, 0.0).astype(BF16)
    lf_all = jnp.concatenate([lf_refs[i][...] for i in range(G)], axis=0)
    within = _dot_exact_r(lf_all, later)
    total = jnp.sum(lf_all, axis=-1, keepdims=True)
    run = carry[...]
    sufs = []
    for i in range(G):
        sufs.append(run + within[i * FOX_HEADS:(i + 1) * FOX_HEADS, :])
        run = run + total[i * FOX_HEADS:(i + 1) * FOX_HEADS, :]
    carry[...] = run
    s = _dot(qbd[...], kt_scr[...]) + expand_heads(jnp.concatenate(sufs, axis=1))
    m = m_scr[...]
    cq = cncol[...]
    m_new = jnp.maximum(m, jnp.max(s, axis=-1, keepdims=True) + cq)
    pr = jnp.exp(s - (m_new - cq))
    alpha = jnp.exp(m - m_new)
    m_scr[...] = m_new
    l_scr[...] = alpha * l_scr[...] + jnp.sum(pr, axis=-1, keepdims=True)
    acc_scr[...] = alpha * acc_scr[...] + _dot_nt(pr.astype(BF16), vt_scr[...])

    @pl.when(p == n_steps - 1)
    def _():
        an = acc_scr[...] / l_scr[...]
        out = jnp.zeros((T, width), F32)
        col_head = _iota((T, width), 1) >> dh_shift
        for h in range(FOX_HEADS):
            out = jnp.where(col_head == h, an[h * T:(h + 1) * T, :], out)
        o_ref[...] = out


def _fox_sample(page_table, q, kn, vn, cnt, cache_k, cache_v, lft):
    bsz, T, width = q.shape
    n_pages = page_table.shape[1]
    G = PAGES_PER_STEP
    n_steps = n_pages // G

    def page(i):
        return lambda b, p, pt: (pt[b, n_pages - 1 - (p * G + i)], 0, 0)

    per_b = lambda s: pl.BlockSpec((None,) + s, lambda b, p, pt: (b, 0, 0))
    in_specs = ([per_b((T, width)), per_b((T, width)), per_b((T, width)), per_b((FOX_HEADS, LANES))]
                + [pl.BlockSpec((None, width, PAGE_SIZE), page(i)) for i in range(G)]
                + [pl.BlockSpec((None, width, PAGE_SIZE), page(i)) for i in range(G)]
                + [pl.BlockSpec((None, FOX_HEADS, PAGE_SIZE), page(i)) for i in range(G)])
    grid_spec = pltpu.PrefetchScalarGridSpec(
        num_scalar_prefetch=1, grid=(bsz, n_steps), in_specs=in_specs, out_specs=per_b((T, width)),
        scratch_shapes=[pltpu.VMEM((LANES, width), BF16), pltpu.VMEM((width, G * PAGE_SIZE), BF16),
                        pltpu.VMEM((width, G * PAGE_SIZE), BF16), pltpu.VMEM((LANES, 1), F32),
                        pltpu.VMEM((LANES, 1), F32), pltpu.VMEM((LANES, width), F32), pltpu.VMEM((FOX_HEADS, 1), F32),
                        pltpu.VMEM((LANES, 1), F32)])
    return pl.pallas_call(
        functools.partial(_fox_sample_kernel, n_steps=n_steps), grid_spec=grid_spec,
        out_shape=jax.ShapeDtypeStruct((bsz, T, width), F32),
        compiler_params=_params(("arbitrary", "arbitrary"), 48), name="fox_sample")(
            page_table, q, kn, vn, cnt, *([cache_k] * G), *([cache_v] * G), *([lft] * G))


def _conf_kernel(u_ref, st0_ref, cw_ref, cb_ref, lg_ref, lb_ref, c_ref, stout_ref, buf, *, chunk, n_chunks):
    t = pl.program_id(1)
    L = chunk
    P0 = CONF_PAD - (CONF_WIDTH - 1)

    @pl.when(t == 0)
    def _():
        buf[P0:CONF_PAD, :] = st0_ref[...]

    u = u_ref[...]
    buf[CONF_PAD:CONF_PAD + L, :] = u[:, :CONF_CH] * _sigmoid(u[:, CONF_CH:])
    cw = cw_ref[...]
    acc = cb_ref[...] + buf[P0:P0 + L, :] * cw[0:1, :]
    for j in range(1, CONF_WIDTH):
        acc = acc + buf[P0 + j:P0 + j + L, :] * cw[j:j + 1, :]
    xc = acc - jnp.mean(acc, axis=-1, keepdims=True)
    var = jnp.mean(xc * xc, axis=-1, keepdims=True)
    c_ref[...] = _silu(xc * lax.rsqrt(var + EPS) * lg_ref[...] + lb_ref[...])

    @pl.when(t == n_chunks - 1)
    def _():
        stout_ref[...] = buf[P0 + L:CONF_PAD + L, :]

    tail = buf[L:L + CONF_PAD, :]
    buf[0:CONF_PAD, :] = tail


def _conf(u, st0, cw, cb, lg, lb, chunk):
    bsz, T, _ = u.shape
    nc = T // chunk
    st = pl.BlockSpec((None, CONF_WIDTH - 1, CONF_CH), lambda b, t: (b, 0, 0))
    return pl.pallas_call(
        functools.partial(_conf_kernel, chunk=chunk, n_chunks=nc), grid=(bsz, nc),
        in_specs=[pl.BlockSpec((None, chunk, 2 * CONF_CH), lambda b, t: (b, t, 0)), st, _resident(cw.shape),
                  _resident(cb.shape), _resident(lg.shape), _resident(lb.shape)],
        out_specs=[pl.BlockSpec((None, chunk, CONF_CH), lambda b, t: (b, t, 0)), st],
        out_shape=[jax.ShapeDtypeStruct((bsz, T, CONF_CH), F32),
                   jax.ShapeDtypeStruct((bsz, CONF_WIDTH - 1, CONF_CH), F32)],
        scratch_shapes=[pltpu.VMEM((CONF_PAD + chunk, CONF_CH), F32)],
        compiler_params=_params(("arbitrary", "arbitrary"), 16), name="conf_conv")(u, st0, cw, cb, lg, lb)


def _pad_cols(w, width):
    return jnp.pad(w, ((0, 0), (0, width - w.shape[1])))


def _split_cols(w, sizes):
    out, off = [], 0
    for s in sizes:
        out.append(w[:, off:off + s])
        off += s
    return out


def _pad_t(a, bsz, t, t_pad):
    a = a.reshape(bsz, t, a.shape[-1])
    return a if t_pad == t else jnp.pad(a, ((0, 0), (0, t_pad - t), (0, 0)))


def _ceil_to(x, m):
    return -(-x // m) * m


def kernel(x_prompt, x_sample, cache_fox_k, cache_fox_v, cache_fox_logf, page_table, state_ssm, state_ssm_conv, state_gla, state_conf_conv, g_mix, g_mlp, g_final, w_in_ab, ssm_conv_w, ssm_conv_b, ssm_dt_bias, ssm_a_log, ssm_d, ssm_norm_g, gla_gate_w2, gla_gate_b, gla_norm_g, w_out_ab, w_in_cd, fox_b_f, conf_conv_w, conf_conv_b, conf_ln_g, conf_ln_b, w_out_cd, w_mlp_up, w_mlp_down):
    row = lambda v: v.reshape(1, -1)
    wz, wxbc, wdt, wq, wk, wv, wg, wlr = _split_cols(w_in_ab[0], AB_SPLIT)
    w_ab = jnp.concatenate([wz, wxbc, wq, wk, wv, wg, _pad_cols(wdt, LANES), _pad_cols(wlr, LANES)], axis=1).astype(BF16)
    cq, ck, cv, cf, cu = _split_cols(w_in_cd[0], CD_SPLIT)
    w_cd = jnp.concatenate([cq, ck, cv, cu, _pad_cols(cf, LANES)], axis=1).astype(BF16)
    w_oab = w_out_ab[0].astype(BF16)
    w_ocd = w_out_cd[0].astype(BF16)
    w_up = w_mlp_up.astype(BF16)
    w_dn = w_mlp_down.astype(BF16)
    dtb = _pad_cols(row(ssm_dt_bias[0]), LANES)
    alog = _pad_cols(row(ssm_a_log[0]), LANES)
    dsk = row(jnp.repeat(ssm_d[0], SSM_HEAD_DIM))
    e16 = (jnp.arange(LANES)[:, None] == jnp.arange(SSM_INNER)[None, :] // SSM_HEAD_DIM).astype(BF16)
    w2 = jnp.pad(gla_gate_w2[0], ((0, LANES - GLA_RANK), (0, 0))).astype(BF16)
    bfp = _pad_cols(row(fox_b_f[0]), LANES)
    width = FOX_HEADS * FOX_HEAD_DIM
    n_pool = cache_fox_k.shape[1]
    cache_k = jnp.transpose(cache_fox_k[0], (0, 2, 3, 1)).reshape(n_pool, width, PAGE_SIZE)
    cache_v = jnp.transpose(cache_fox_v[0], (0, 2, 3, 1)).reshape(n_pool, width, PAGE_SIZE)
    lft = jnp.swapaxes(cache_fox_logf[0], 1, 2)

    def trunk(x3, sample):
        bsz, T, _ = x3.shape
        n = bsz * T
        tm = min(n, 256)
        tm_big = min(n, 512)
        x = x3.reshape(n, D_MODEL)
        if sample:
            h0 = state_ssm[0].reshape(bsz, SSM_INNER, SSM_STATE)
            conv0, s0, conf0 = state_ssm_conv[0], state_gla[0].reshape(bsz, GLA_HEADS * GLA_DK, GLA_DV), state_conf_conv[0]
        else:
            h0 = jnp.zeros((bsz, SSM_INNER, SSM_STATE), F32)
            conv0 = jnp.zeros((bsz, SSM_CONV - 1, SSM_CONV_CH), F32)
            s0 = jnp.zeros((bsz, GLA_HEADS * GLA_DK, GLA_DV), F32)
            conf0 = jnp.zeros((bsz, CONF_WIDTH - 1, CONF_CH), F32)

        z, xbc, q, k, v, g, dtp, glr = _norm_proj(x, row(g_mix[0]), w_ab, AB_SECTIONS, tm)
        ta = _ceil_to(T, SSD_CHUNK)
        y, h_new, conv_new = _ssd(_pad_t(z, bsz, T, ta), _pad_t(xbc, bsz, T, ta), _pad_t(dtp, bsz, T, ta), conv0, h0,
                                  ssm_conv_w[0], row(ssm_conv_b[0]), dtb, alog, dsk, row(ssm_norm_g[0]), e16, T)
        tb = _ceil_to(T, GLA_CHUNK)
        o, s_new = _gla(_pad_t(q, bsz, T, tb), _pad_t(k, bsz, T, tb), _pad_t(v, bsz, T, tb), _pad_t(g, bsz, T, tb),
                        _pad_t(glr, bsz, T, tb), s0, w2, row(gla_gate_b[0]), row(gla_norm_g[0]), T)
        y = y[:, :T].reshape(n, SSM_INNER)
        o = o[:, :T].reshape(n, GLA_HEADS * GLA_DV)
        x = _out_proj(x, y, o, w_oab, tm_big)
        x = _mlp(x, row(g_mlp[0]), w_up[0], w_dn[0], row(g_final), tm_big, False)

        tg = _ceil_to(T, GATE_CHUNK)
        head_shape = (1, bsz, T, FOX_HEADS, FOX_HEAD_DIM)
        if sample:
            q, k, v, u, f = _norm_proj(x, row(g_mix[1]), w_cd, CD_SECTIONS, tm)
            lf_t, c, ct = _fox_gate(_pad_t(f, bsz, T, tg), bfp)
            q3, k3, v3 = (a.reshape(bsz, T, width) for a in (q, k, v))
            att = _fox_sample(page_table, q3, k3, v3, ct, cache_k, cache_v, lft)
            k_out, v_out = k.reshape(head_shape), v.reshape(head_shape)
        else:
            q, k, kt, vt, u, f = _norm_proj(x, row(g_mix[1]), w_cd, CD_SECTIONS, tm, ("n", "nt", "t", "n", "n"), T)
            lf_t, c, ct = _fox_gate(_pad_t(f, bsz, T, tg), bfp)
            att = _fox_prompt(q.reshape(bsz, T, width), k.reshape(bsz, T, width), vt, c, ct)
            to_rows = lambda a: jnp.transpose(a.reshape(1, bsz, FOX_HEADS, FOX_HEAD_DIM, T), (0, 1, 4, 2, 3))
            k_out, v_out = to_rows(kt), to_rows(vt)
        lf = jnp.swapaxes(lf_t, 1, 2)[:, :T]
        cmod, conf_new = _conf(u.reshape(bsz, T, 2 * CONF_CH), conf0, conf_conv_w[0], row(conf_conv_b[0]),
                               row(conf_ln_g[0]), row(conf_ln_b[0]), min(T, CONF_CHUNK))
        x = _out_proj(x, att.reshape(n, width), cmod.reshape(n, CONF_CH), w_ocd, tm_big)
        x = _mlp(x, row(g_mlp[1]), w_up[1], w_dn[1], row(g_final), tm_big, True)
        return (x.reshape(bsz, T, D_MODEL), h_new.reshape(1, bsz, SSM_HEADS, SSM_HEAD_DIM, SSM_STATE), conv_new[None],
                s_new.reshape(1, bsz, GLA_HEADS, GLA_DK, GLA_DV), k_out, v_out, lf[None], conf_new[None])

    yp, *rest_p = trunk(x_prompt, False)
    ys, *rest_s = trunk(x_sample, True)
    return (yp, ys, *rest_p, *rest_s)
```

```python
import functools

import jax
import jax.numpy as jnp
from jax import lax
from jax.experimental import pallas as pl
from jax.experimental.pallas import tpu as pltpu

F32 = jnp.float32
BF16 = jnp.bfloat16
EPS = 1e-6
LOG2E = 1.4426950408889634

LANES = 128
SUBLANES = 8
MIB = 1024 * 1024

D_MODEL = 1024
D_FF = 4 * D_MODEL
SSM_HEADS = 16
SSM_HEAD_DIM = 64
SSM_INNER = SSM_HEADS * SSM_HEAD_DIM
SSM_GROUPS = 2
SSM_STATE = 128
SSM_CONV = 4
SSM_CONV_CH = SSM_INNER + 2 * SSM_GROUPS * SSM_STATE
GLA_HEADS = 4
GLA_DK = 128
GLA_DV = 256
GLA_RANK = 16
GLA_GATE_NORM = 16.0
FOX_HEADS = 16
FOX_HEAD_DIM = 64
CONF_CH = 512
CONF_WIDTH = 31
PAGE_SIZE = 128

AB_SPLIT = (SSM_INNER, SSM_CONV_CH, SSM_HEADS, GLA_HEADS * GLA_DK, GLA_HEADS * GLA_DK,
            GLA_HEADS * GLA_DV, GLA_HEADS * GLA_DV, GLA_RANK)
CD_SPLIT = (FOX_HEADS * FOX_HEAD_DIM,) * 3 + (FOX_HEADS, 2 * CONF_CH)
AB_SECTIONS = (SSM_INNER, SSM_CONV_CH, GLA_HEADS * GLA_DK, GLA_HEADS * GLA_DK, GLA_HEADS * GLA_DV,
               GLA_HEADS * GLA_DV, LANES, LANES)
CD_SECTIONS = (1024, 1024, 1024, 2 * CONF_CH, LANES)

SSD_CHUNK = 128
GLA_CHUNK = 64
GLA_STEP = 128
GATE_CHUNK = 128
FOX_TQ = 256
FOX_TK = 256
CONF_CHUNK = 256
CONF_PAD = 32
CONF_ROWS = 32
SSD_PAD = 8
FF_CHUNK = 1024
PAGES_PER_STEP = 16


def _dot(a, b):
    return jnp.dot(a, b, preferred_element_type=F32)


def _dot_nt(a, b):
    return lax.dot_general(a, b, (((1,), (1,)), ((), ())), preferred_element_type=F32)


def _dot_tn(a, b):
    return lax.dot_general(a, b, (((0,), (0,)), ((), ())), preferred_element_type=F32)


def _split3(x):
    hi = x.astype(BF16)
    r1 = x - hi.astype(F32)
    mid = r1.astype(BF16)
    lo = (r1 - mid.astype(F32)).astype(BF16)
    return hi, mid, lo


def _dot_exact_l(m, x):
    hi, mid, lo = _split3(x)
    return _dot(m, lo) + _dot(m, mid) + _dot(m, hi)


def _dot_exact_r(x, m):
    hi, mid, lo = _split3(x)
    return _dot(lo, m) + _dot(mid, m) + _dot(hi, m)


def _sigmoid(x):
    return 1.0 / (1.0 + jnp.exp(-x))


def _silu(x):
    return x * _sigmoid(x)


def _softplus(x):
    return jnp.maximum(x, 0.0) + jnp.log1p(jnp.exp(-jnp.abs(x)))


def _log_sigmoid(x):
    return jnp.minimum(x, 0.0) - jnp.log1p(jnp.exp(-jnp.abs(x)))


def _rms(x, g):
    return x * lax.rsqrt(jnp.mean(x * x, axis=-1, keepdims=True) + EPS) * g


def _iota(shape, dim):
    return lax.broadcasted_iota(jnp.int32, shape, dim)


def _tril(n):
    return _iota((n, n), 1) <= _iota((n, n), 0)


def _params(semantics, vmem_mib):
    return pltpu.CompilerParams(dimension_semantics=semantics, vmem_limit_bytes=vmem_mib * MIB)


def _resident(shape):
    nd = len(shape)
    return pl.BlockSpec(shape, lambda *_: (0,) * nd, pipeline_mode=pl.Buffered(1))


def _norm_proj_kernel(x_ref, g_ref, w_ref, *out_refs, sections):
    xb = _rms(x_ref[...], g_ref[...]).astype(BF16)
    outs = iter(out_refs)
    for off, width, mode in sections:
        y = _dot(xb, w_ref[:, off:off + width])
        if "n" in mode:
            next(outs)[...] = y
        if "t" in mode:
            next(outs)[...] = y.T


def _norm_proj(x, g, w, widths, tm, modes=None, seq=None):
    n = x.shape[0]
    modes = modes or ("n",) * len(widths)
    offs = [sum(widths[:i]) for i in range(len(widths))]
    kern = functools.partial(_norm_proj_kernel, sections=tuple(zip(offs, widths, modes)))
    wtot = w.shape[1]
    out_specs, out_shape = [], []
    for s, mode in zip(widths, modes):
        if "n" in mode:
            out_specs.append(pl.BlockSpec((tm, s), lambda i: (i, 0)))
            out_shape.append(jax.ShapeDtypeStruct((n, s), F32))
        if "t" in mode:
            per_seq = seq // tm
            out_specs.append(pl.BlockSpec((None, s, tm), lambda i: (i // per_seq, 0, i % per_seq)))
            out_shape.append(jax.ShapeDtypeStruct((n // seq, s, seq), F32))
    n_out = sum(s * len(mode) for s, mode in zip(widths, modes))
    vmem = (2 * tm * D_MODEL * 4 + D_MODEL * wtot * 2 + 3 * tm * n_out * 4) // MIB + 4
    return pl.pallas_call(
        kern, grid=(n // tm,),
        in_specs=[pl.BlockSpec((tm, D_MODEL), lambda i: (i, 0)), _resident((1, D_MODEL)), _resident(w.shape)],
        out_specs=out_specs, out_shape=out_shape,
        compiler_params=_params(("arbitrary",), vmem), name="norm_proj")(x, g, w)


def _ssd_kernel(z_ref, xbc_ref, dt_ref, conv0_ref, h0_ref, cw_ref, cb_ref, dtb_ref, alog_ref, dsk_ref, ng_ref,
                e16_ref, y_ref, hout_ref, convout_ref, xbuf, h_scr, y_scr, *, n_chunks, t_valid):
    c = pl.program_id(1)
    L = SSD_CHUNK
    P0 = SSD_PAD - (SSM_CONV - 1)
    last_valid = t_valid - (n_chunks - 1) * L

    @pl.when(c == 0)
    def _():
        xbuf[P0:SSD_PAD, :] = conv0_ref[...]
        h_scr[...] = h0_ref[...]

    xbuf[SSD_PAD:SSD_PAD + L, :] = xbc_ref[...]
    cw = cw_ref[...]
    conv = cb_ref[...] + xbuf[P0:P0 + L, :] * cw[0:1, :]
    for j in range(1, SSM_CONV):
        conv = conv + xbuf[P0 + j:P0 + j + L, :] * cw[j:j + 1, :]
    act = _silu(conv)
    xs = act[:, :SSM_INNER]
    bm_b = act[:, SSM_INNER:SSM_INNER + SSM_GROUPS * SSM_STATE].astype(BF16)
    cm_b = act[:, SSM_INNER + SSM_GROUPS * SSM_STATE:].astype(BF16)

    row = _iota((L, LANES), 0)
    lane = _iota((L, LANES), 1)
    live = (lane < SSM_HEADS) & (c * L + row < t_valid)
    dt = jnp.where(live, _softplus(dt_ref[...] + dtb_ref[...]), 0.0)
    a = -jnp.exp(alog_ref[...]) * dt
    causal = _tril(L)
    tri = jnp.where(causal, 1.0, 0.0).astype(BF16)
    ac = _dot_exact_l(tri, a)
    e16 = e16_ref[...]
    ac_x = _dot_exact_r(ac, e16)
    dt_x = _dot_exact_r(dt, e16)
    ac_t = ac.T
    ac_last = ac[L - 1:L, :]
    ac_last_x = ac_x[L - 1:L, :]
    to_end_x = jnp.exp(ac_last_x - ac_x)
    eac_x = jnp.exp(ac_x)
    xdt = xs * dt_x
    xdt_b = xdt.astype(BF16)
    xend_b = (xdt * to_end_x).astype(BF16)
    lane_lo = lane < SSM_HEAD_DIM
    row_lo = _iota((LANES, LANES), 0) < SSM_HEAD_DIM
    heads_per_group = SSM_HEADS // SSM_GROUPS
    for g in range(SSM_GROUPS):
        bg = bm_b[:, g * SSM_STATE:(g + 1) * SSM_STATE]
        cg = cm_b[:, g * SSM_STATE:(g + 1) * SSM_STATE]
        cb = _dot_nt(cg, bg)
        for p in range(g * heads_per_group // 2, (g + 1) * heads_per_group // 2):
            sl = slice(p * LANES, (p + 1) * LANES)
            ys = []
            for j in range(2):
                h = 2 * p + j
                seg = ac[:, h:h + 1] - ac_t[h:h + 1, :]
                m = (cb * jnp.exp(jnp.where(causal, seg, -jnp.inf))).astype(BF16)
                ys.append(_dot(m, xdt_b[:, sl]))
            y_diag = jnp.where(lane_lo, ys[0], ys[1])
            s_old = h_scr[sl, :]
            y_off = _dot_nt(cg, s_old.astype(BF16)) * eac_x[:, sl]
            dec = jnp.exp(jnp.where(row_lo, ac_last[:, 2 * p:2 * p + 1], ac_last[:, 2 * p + 1:2 * p + 2]))
            h_scr[sl, :] = s_old * dec + _dot_tn(xend_b[:, sl], bg)
            y_scr[:, sl] = y_diag + y_off + dsk_ref[:, sl] * xs[:, sl]

    y_ref[...] = _rms(y_scr[...] * _silu(z_ref[...]), ng_ref[...])

    @pl.when(c == n_chunks - 1)
    def _():
        hout_ref[...] = h_scr[...]
        convout_ref[...] = xbuf[P0 + last_valid:SSD_PAD + last_valid, :]

    tail = xbuf[P0 + L:SSD_PAD + L, :]
    xbuf[P0:SSD_PAD, :] = tail


def _ssd(z, xbc, dtp, conv0, h0, cw, cb, dtb, alog, dsk, ng, e16, t_valid):
    bsz, t_pad, _ = z.shape
    L = SSD_CHUNK
    nc = t_pad // L
    kern = functools.partial(_ssd_kernel, n_chunks=nc, t_valid=t_valid)
    tok = lambda w: pl.BlockSpec((None, L, w), lambda b, c: (b, c, 0))
    per_b = lambda s: pl.BlockSpec((None,) + s, lambda b, c: (b, 0, 0))
    return pl.pallas_call(
        kern, grid=(bsz, nc),
        in_specs=[tok(SSM_INNER), tok(SSM_CONV_CH), tok(LANES), per_b((SSM_CONV - 1, SSM_CONV_CH)),
                  per_b((SSM_INNER, SSM_STATE)), _resident(cw.shape), _resident(cb.shape), _resident(dtb.shape),
                  _resident(alog.shape), _resident(dsk.shape), _resident(ng.shape), _resident(e16.shape)],
        out_specs=[tok(SSM_INNER), per_b((SSM_INNER, SSM_STATE)), per_b((SSM_CONV - 1, SSM_CONV_CH))],
        out_shape=[jax.ShapeDtypeStruct((bsz, t_pad, SSM_INNER), F32),
                   jax.ShapeDtypeStruct((bsz, SSM_INNER, SSM_STATE), F32),
                   jax.ShapeDtypeStruct((bsz, SSM_CONV - 1, SSM_CONV_CH), F32)],
        scratch_shapes=[pltpu.VMEM((SSD_PAD + L, SSM_CONV_CH), F32), pltpu.VMEM((SSM_INNER, SSM_STATE), F32),
                        pltpu.VMEM((L, SSM_INNER), F32)],
        compiler_params=_params(("arbitrary", "arbitrary"), 40), name="ssd")(
            z, xbc, dtp, conv0, h0, cw, cb, dtb, alog, dsk, ng, e16)


def _gla_kernel(q_ref, k_ref, v_ref, g_ref, glr_ref, s0_ref, w2_ref, gb_ref, ng_ref, o_ref, sout_ref, s_scr,
                *, n_chunks, t_valid):
    c = pl.program_id(1)
    R, L = GLA_STEP, GLA_CHUNK
    n_sub = R // L
    chunk_shift = L.bit_length() - 1
    width = GLA_HEADS * GLA_DK

    @pl.when(c == 0)
    def _():
        s_scr[...] = s0_ref[...]

    x = _dot(glr_ref[...].astype(BF16), w2_ref[...]) + gb_ref[...]
    logf = _log_sigmoid(x) * (1.0 / GLA_GATE_NORM)
    logf = jnp.where(c * R + _iota((R, width), 0) < t_valid, logf, 0.0)
    row, col = _iota((R, R), 0), _iota((R, R), 1)
    causal = ((row >> chunk_shift) == (col >> chunk_shift)) & (col <= row)
    tri = jnp.where(causal, 1.0, 0.0).astype(BF16)
    bcum = _dot_exact_l(tri, logf)
    bl = [bcum[(s + 1) * L - 1:(s + 1) * L, :] for s in range(n_sub)]
    bl_rows = jnp.concatenate([jnp.broadcast_to(b, (L, width)) for b in bl], axis=0)
    q_dec = q_ref[...] * (GLA_DK ** -0.5) * jnp.exp(bcum)
    k = k_ref[...]
    k_inv = k * jnp.exp(-bcum)
    k_end = k * jnp.exp(bl_rows - bcum)
    for h in range(GLA_HEADS):
        ks = slice(h * GLA_DK, (h + 1) * GLA_DK)
        vs = slice(h * GLA_DV, (h + 1) * GLA_DV)
        qd = q_dec[:, ks].astype(BF16)
        ke = k_end[:, ks].astype(BF16)
        vb = v_ref[:, vs].astype(BF16)
        att = jnp.where(causal, _dot_nt(qd, k_inv[:, ks].astype(BF16)), 0.0)
        o = _dot(att.astype(BF16), vb)
        state = s_scr[ks, :]
        inter = []
        for s in range(n_sub):
            rs = slice(s * L, (s + 1) * L)
            inter.append(_dot(qd[rs], state.astype(BF16)))
            dcol = jnp.broadcast_to(jnp.exp(bl[s][:, ks]), (GLA_DK, GLA_DK)).T
            state = state * jnp.concatenate([dcol, dcol], axis=1) + _dot_tn(ke[rs], vb[rs])
        s_scr[ks, :] = state
        o = o + jnp.concatenate(inter, axis=0)
        o_ref[:, vs] = _rms(o, ng_ref[...]) * _silu(g_ref[:, vs])

    @pl.when(c == n_chunks - 1)
    def _():
        sout_ref[...] = s_scr[...]


def _gla(q, k, v, g, glr, s0, w2, gb, ng, t_valid):
    bsz, t_pad, _ = q.shape
    L = GLA_STEP
    nc = t_pad // L
    kern = functools.partial(_gla_kernel, n_chunks=nc, t_valid=t_valid)
    tok = lambda w: pl.BlockSpec((None, L, w), lambda b, c: (b, c, 0))
    st = pl.BlockSpec((None, GLA_HEADS * GLA_DK, GLA_DV), lambda b, c: (b, 0, 0))
    return pl.pallas_call(
        kern, grid=(bsz, nc),
        in_specs=[tok(GLA_HEADS * GLA_DK), tok(GLA_HEADS * GLA_DK), tok(GLA_HEADS * GLA_DV), tok(GLA_HEADS * GLA_DV),
                  tok(LANES), st, _resident(w2.shape), _resident(gb.shape), _resident(ng.shape)],
        out_specs=[tok(GLA_HEADS * GLA_DV), st],
        out_shape=[jax.ShapeDtypeStruct((bsz, t_pad, GLA_HEADS * GLA_DV), F32),
                   jax.ShapeDtypeStruct((bsz, GLA_HEADS * GLA_DK, GLA_DV), F32)],
        scratch_shapes=[pltpu.VMEM((GLA_HEADS * GLA_DK, GLA_DV), F32)],
        compiler_params=_params(("arbitrary", "arbitrary"), 32), name="gla")(q, k, v, g, glr, s0, w2, gb, ng)


def _out_proj_kernel(x_ref, a_ref, b_ref, w_ref, o_ref, *, wa):
    o_ref[...] = (x_ref[...] + _dot(a_ref[...].astype(BF16), w_ref[:wa, :])
                  + _dot(b_ref[...].astype(BF16), w_ref[wa:, :]))


def _out_proj(x, a, b, w, tm):
    n = x.shape[0]
    wa, wb = a.shape[1], b.shape[1]
    row = lambda s: pl.BlockSpec((tm, s), lambda i: (i, 0))
    return pl.pallas_call(
        functools.partial(_out_proj_kernel, wa=wa), grid=(n // tm,),
        in_specs=[row(D_MODEL), row(wa), row(wb), _resident(w.shape)],
        out_specs=row(D_MODEL), out_shape=jax.ShapeDtypeStruct((n, D_MODEL), F32),
        compiler_params=_params(("arbitrary",), 40), name="out_proj")(x, a, b, w)


def _mlp_kernel(x_ref, g_ref, wu_ref, wd_ref, gf_ref, o_ref, *, final_norm):
    x = x_ref[...]
    xb = _rms(x, g_ref[...]).astype(BF16)
    acc = x
    for c in range(D_FF // FF_CHUNK):
        sl = slice(c * FF_CHUNK, (c + 1) * FF_CHUNK)
        h = jnp.maximum(_dot(xb, wu_ref[:, sl]), 0.0)
        acc = acc + _dot((h * h).astype(BF16), wd_ref[sl, :])
    o_ref[...] = _rms(acc, gf_ref[...]) if final_norm else acc


def _mlp(x, g, wu, wd, gf, tm, final_norm):
    n = x.shape[0]
    row = pl.BlockSpec((tm, D_MODEL), lambda i: (i, 0))
    return pl.pallas_call(
        functools.partial(_mlp_kernel, final_norm=final_norm), grid=(n // tm,),
        in_specs=[row, _resident(g.shape), _resident(wu.shape), _resident(wd.shape), _resident(gf.shape)],
        out_specs=row, out_shape=jax.ShapeDtypeStruct((n, D_MODEL), F32),
        compiler_params=_params(("arbitrary",), 48), name="mlp")(x, g, wu, wd, gf)


def _fox_gate_kernel(f_ref, bf_ref, lft_ref, ct_ref, carry):
    @pl.when(pl.program_id(1) == 0)
    def _():
        carry[...] = jnp.zeros_like(carry)

    L = GATE_CHUNK
    lf = _log_sigmoid(f_ref[...] + bf_ref[...])
    tri = jnp.where(_tril(L), 1.0, 0.0).astype(BF16)
    cblk = carry[...] + _dot_exact_l(tri, lf)
    carry[...] = cblk[L - 1:L, :]
    lft_ref[...] = lf.T[:FOX_HEADS, :]
    ct_ref[...] = cblk.T[:FOX_HEADS, :]


def _fox_gate(f, bf):
    bsz, t_pad, _ = f.shape
    L = GATE_CHUNK
    return pl.pallas_call(
        _fox_gate_kernel, grid=(bsz, t_pad // L),
        in_specs=[pl.BlockSpec((None, L, LANES), lambda b, t: (b, t, 0)), _resident(bf.shape)],
        out_specs=[pl.BlockSpec((None, FOX_HEADS, L), lambda b, t: (b, 0, t)),
                   pl.BlockSpec((None, FOX_HEADS, L), lambda b, t: (b, 0, t))],
        out_shape=[jax.ShapeDtypeStruct((bsz, FOX_HEADS, t_pad), F32),
                   jax.ShapeDtypeStruct((bsz, FOX_HEADS, t_pad), F32)],
        scratch_shapes=[pltpu.VMEM((1, LANES), F32)],
        compiler_params=_params(("arbitrary", "arbitrary"), 16), name="fox_gate")(f, bf)


def _fox_prompt_kernel(q_ref, k_ref, vt_ref, ct_ref, o_ref, kb_scr, vtb_scr, ck_scr, s_scr, p_scr, *,
                       n_q_blocks):
    hp = pl.program_id(1)
    tq, tk = FOX_TQ, FOX_TK
    dh = FOX_HEAD_DIM
    heads = [2 * hp, 2 * hp + 1]

    kb_scr[...] = k_ref[...].astype(BF16)
    vtb_scr[...] = vt_ref[...].astype(BF16)
    for j in range(2):
        c_row = ct_ref[pl.ds(heads[j], 1), :] * LOG2E
        ck_scr[j] = jnp.broadcast_to(c_row, (LANES, c_row.shape[1])).T

    lane = _iota((tq, LANES), 1)
    causal = _iota((tk, tq), 0) <= _iota((tk, tq), 1)

    def fold8(x, op):
        parts = [x[r * 8:(r + 1) * 8, :] for r in range(x.shape[0] // 8)]
        while len(parts) > 1:
            parts = [op(parts[i], parts[i + 1]) for i in range(0, len(parts), 2)]
        return parts[0]

    for v in range(n_q_blocks):
        n_blocks = v + 1
        n_k = n_blocks * tk
        q = q_ref[v * tq:(v + 1) * tq, :] * (dh ** -0.5 * LOG2E)
        qm = [jnp.where(lane < dh, q, 0.0).astype(BF16), jnp.where(lane >= dh, q, 0.0).astype(BF16)]
        outs = []
        for j in range(2):
            slot = 2 * (v % 2) + j
            cq = ct_ref[pl.ds(heads[j], 1), v * tq:(v + 1) * tq] * LOG2E
            mx = jnp.full((8, tq), -jnp.inf, F32)
            for ki in range(n_blocks):
                rows = slice(ki * tk, (ki + 1) * tk)
                s = _dot_nt(kb_scr[rows, :], qm[j]) - jnp.concatenate([ck_scr[j, rows, :]] * (tq // LANES), axis=1)
                if ki == n_blocks - 1:
                    s = jnp.where(causal, s, -jnp.inf)
                s_scr[slot, rows, :] = s
                mx = jnp.maximum(mx, fold8(s, jnp.maximum))
            m_new = jnp.max(mx, axis=0, keepdims=True) + cq
            shift = m_new - cq
            lsum = jnp.zeros((8, tq), F32)
            for ki in range(n_blocks):
                rows = slice(ki * tk, (ki + 1) * tk)
                p = jnp.exp2(s_scr[slot, rows, :] - shift)
                lsum = lsum + fold8(p, jnp.add)
                p_scr[slot, rows, :] = p.astype(BF16)
            acc = _dot(vtb_scr[j * dh:(j + 1) * dh, 0:n_k], p_scr[slot, 0:n_k, :])
            outs.append(acc / jnp.sum(lsum, axis=0, keepdims=True))
        o_ref[v * tq:(v + 1) * tq, :] = jnp.concatenate(outs, axis=0).T


def _fox_prompt(q, k, vt, ct):
    bsz, T, _ = q.shape
    tq = FOX_TQ
    return pl.pallas_call(
        functools.partial(_fox_prompt_kernel, n_q_blocks=T // tq), grid=(bsz, FOX_HEADS // 2),
        in_specs=[pl.BlockSpec((None, T, LANES), lambda b, hp: (b, 0, hp)),
                  pl.BlockSpec((None, T, LANES), lambda b, hp: (b, 0, hp)),
                  pl.BlockSpec((None, LANES, T), lambda b, hp: (b, hp, 0)),
                  pl.BlockSpec((None, FOX_HEADS, T), lambda b, hp: (b, 0, 0))],
        out_specs=pl.BlockSpec((None, T, LANES), lambda b, hp: (b, 0, hp)),
        out_shape=jax.ShapeDtypeStruct((bsz, T, FOX_HEADS * FOX_HEAD_DIM), F32),
        scratch_shapes=[pltpu.VMEM((T, LANES), BF16), pltpu.VMEM((LANES, T), BF16), pltpu.VMEM((2, T, LANES), F32),
                        pltpu.VMEM((4, T, tq), F32), pltpu.VMEM((4, T, tq), BF16)],
        compiler_params=_params(("arbitrary",) * 2, 40), name="fox_prompt")(q, k, vt, ct)


def _fox_sample_kernel(pt_ref, q_ref, kn_ref, vn_ref, cnt_ref, *refs, n_steps):
    G = PAGES_PER_STEP
    k_refs, v_refs, lf_refs = refs[:G], refs[G:2 * G], refs[2 * G:3 * G]
    o_ref, qbd, kt_scr, vt_scr, m_scr, l_scr, acc_scr, carry, cncol = refs[3 * G:]
    p = pl.program_id(1)
    R = LANES
    T = R // FOX_HEADS
    width = FOX_HEADS * FOX_HEAD_DIM
    dh_shift = FOX_HEAD_DIM.bit_length() - 1
    t_shift = T.bit_length() - 1
    row = _iota((R, LANES), 0)
    lane = _iota((R, LANES), 1)

    def expand_heads(xt):
        return jnp.concatenate([jnp.broadcast_to(xt[h:h + 1, :], (T, xt.shape[1])) for h in range(FOX_HEADS)], axis=0)

    def update(s, vb):
        m = m_scr[...]
        cq = cncol[...]
        m_new = jnp.maximum(m, jnp.max(s, axis=-1, keepdims=True) + cq)
        pr = jnp.exp(s - (m_new - cq))
        alpha = jnp.exp(m - m_new)
        m_scr[...] = m_new
        l_scr[...] = alpha * l_scr[...] + jnp.sum(pr, axis=-1, keepdims=True)
        acc_scr[...] = alpha * acc_scr[...] + _dot(pr.astype(BF16), vb)

    @pl.when(p == 0)
    def _():
        q = q_ref[...] * (FOX_HEAD_DIM ** -0.5)
        qt = jnp.concatenate([q] * FOX_HEADS, axis=0)
        own = (_iota((R, width), 1) >> dh_shift) == (_iota((R, width), 0) >> t_shift)
        qbd[...] = jnp.where(own, qt, 0.0).astype(BF16)
        m_scr[...] = jnp.full_like(m_scr, -jnp.inf)
        l_scr[...] = jnp.zeros_like(l_scr)
        acc_scr[...] = jnp.zeros_like(acc_scr)
        carry[...] = jnp.zeros_like(carry)
        cn = expand_heads(cnt_ref[...])
        t_of_row = row & (T - 1)
        cncol[...] = jnp.sum(jnp.where(lane == t_of_row, cn, 0.0), axis=-1, keepdims=True)
        pad = jnp.zeros((LANES - T, width), F32)
        kb = jnp.concatenate([kn_ref[...], pad], axis=0).astype(BF16)
        vb = jnp.concatenate([vn_ref[...], pad], axis=0).astype(BF16)
        s = _dot_nt(qbd[...], kb) - cn
        update(jnp.where(lane <= t_of_row, s, -jnp.inf), vb)

    for i in range(G):
        kt_scr[:, i * PAGE_SIZE:(i + 1) * PAGE_SIZE] = k_refs[i][...].astype(BF16)
        vt_scr[:, i * PAGE_SIZE:(i + 1) * PAGE_SIZE] = v_refs[i][...].astype(BF16)
    later = jnp.where(_iota((LANES, LANES), 0) > _iota((LANES, LANES), 1), 1.0, 0.0).astype(BF16)
    lf_all = jnp.concatenate([lf_refs[i][...] for i in range(G)], axis=0)
    within = _dot_exact_r(lf_all, later)
    total = jnp.sum(lf_all, axis=-1, keepdims=True)
    run = carry[...]
    sufs = []
    for i in range(G):
        sufs.append(run + within[i * FOX_HEADS:(i + 1) * FOX_HEADS, :])
        run = run + total[i * FOX_HEADS:(i + 1) * FOX_HEADS, :]
    carry[...] = run
    s = _dot(qbd[...], kt_scr[...]) + expand_heads(jnp.concatenate(sufs, axis=1))
    m = m_scr[...]
    cq = cncol[...]
    m_new = jnp.maximum(m, jnp.max(s, axis=-1, keepdims=True) + cq)
    pr = jnp.exp(s - (m_new - cq))
    alpha = jnp.exp(m - m_new)
    m_scr[...] = m_new
    l_scr[...] = alpha * l_scr[...] + jnp.sum(pr, axis=-1, keepdims=True)
    acc_scr[...] = alpha * acc_scr[...] + _dot_nt(pr.astype(BF16), vt_scr[...])

    @pl.when(p == n_steps - 1)
    def _():
        an = acc_scr[...] / l_scr[...]
        out = jnp.zeros((T, width), F32)
        col_head = _iota((T, width), 1) >> dh_shift
        for h in range(FOX_HEADS):
            out = jnp.where(col_head == h, an[h * T:(h + 1) * T, :], out)
        o_ref[...] = out


def _fox_sample(page_table, q, kn, vn, cnt, cache_k, cache_v, lft):
    bsz, T, width = q.shape
    n_pages = page_table.shape[1]
    G = PAGES_PER_STEP
    n_steps = n_pages // G

    def page(i):
        return lambda b, p, pt: (pt[b, n_pages - 1 - (p * G + i)], 0, 0)

    per_b = lambda s: pl.BlockSpec((None,) + s, lambda b, p, pt: (b, 0, 0))
    in_specs = ([per_b((T, width)), per_b((T, width)), per_b((T, width)), per_b((FOX_HEADS, LANES))]
                + [pl.BlockSpec((None, width, PAGE_SIZE), page(i)) for i in range(G)]
                + [pl.BlockSpec((None, width, PAGE_SIZE), page(i)) for i in range(G)]
                + [pl.BlockSpec((None, FOX_HEADS, PAGE_SIZE), page(i)) for i in range(G)])
    grid_spec = pltpu.PrefetchScalarGridSpec(
        num_scalar_prefetch=1, grid=(bsz, n_steps), in_specs=in_specs, out_specs=per_b((T, width)),
        scratch_shapes=[pltpu.VMEM((LANES, width), BF16), pltpu.VMEM((width, G * PAGE_SIZE), BF16),
                        pltpu.VMEM((width, G * PAGE_SIZE), BF16), pltpu.VMEM((LANES, 1), F32),
                        pltpu.VMEM((LANES, 1), F32), pltpu.VMEM((LANES, width), F32), pltpu.VMEM((FOX_HEADS, 1), F32),
                        pltpu.VMEM((LANES, 1), F32)])
    return pl.pallas_call(
        functools.partial(_fox_sample_kernel, n_steps=n_steps), grid_spec=grid_spec,
        out_shape=jax.ShapeDtypeStruct((bsz, T, width), F32),
        compiler_params=_params(("arbitrary", "arbitrary"), 56), name="fox_sample")(
            page_table, q, kn, vn, cnt, *([cache_k] * G), *([cache_v] * G), *([lft] * G))


def _conf_kernel(u_ref, st0_ref, cw_ref, cb_ref, lg_ref, lb_ref, c_ref, stout_ref, buf, sh, *, chunk, n_chunks):
    t = pl.program_id(1)
    L = chunk
    P0 = CONF_PAD - (CONF_WIDTH - 1)

    @pl.when(t == 0)
    def _():
        buf[P0:CONF_PAD, :] = st0_ref[...]

    u = u_ref[...]
    buf[CONF_PAD:CONF_PAD + L, :] = u[:, :CONF_CH] * _sigmoid(u[:, CONF_CH:])
    span = L + CONF_PAD - SUBLANES
    for b in range(1, SUBLANES):
        sh[b - 1, 0:span, :] = buf[b:b + span, :]
    cw = cw_ref[...]
    rb = min(L, CONF_ROWS)
    for r0 in range(0, L, rb):
        acc = jnp.broadcast_to(cb_ref[...], (rb, CONF_CH))
        for j in range(CONF_WIDTH):
            a, b = divmod(P0 + j, SUBLANES)
            lo = r0 + a * SUBLANES
            src = buf[lo:lo + rb, :] if b == 0 else sh[b - 1, lo:lo + rb, :]
            acc = acc + src * cw[j:j + 1, :]
        xc = acc - jnp.mean(acc, axis=-1, keepdims=True)
        var = jnp.mean(xc * xc, axis=-1, keepdims=True)
        c_ref[r0:r0 + rb, :] = _silu(xc * lax.rsqrt(var + EPS) * lg_ref[...] + lb_ref[...])

    @pl.when(t == n_chunks - 1)
    def _():
        stout_ref[...] = buf[P0 + L:CONF_PAD + L, :]

    tail = buf[L:L + CONF_PAD, :]
    buf[0:CONF_PAD, :] = tail


def _conf(u, st0, cw, cb, lg, lb, chunk):
    bsz, T, _ = u.shape
    nc = T // chunk
    st = pl.BlockSpec((None, CONF_WIDTH - 1, CONF_CH), lambda b, t: (b, 0, 0))
    return pl.pallas_call(
        functools.partial(_conf_kernel, chunk=chunk, n_chunks=nc), grid=(bsz, nc),
        in_specs=[pl.BlockSpec((None, chunk, 2 * CONF_CH), lambda b, t: (b, t, 0)), st, _resident(cw.shape),
                  _resident(cb.shape), _resident(lg.shape), _resident(lb.shape)],
        out_specs=[pl.BlockSpec((None, chunk, CONF_CH), lambda b, t: (b, t, 0)), st],
        out_shape=[jax.ShapeDtypeStruct((bsz, T, CONF_CH), F32),
                   jax.ShapeDtypeStruct((bsz, CONF_WIDTH - 1, CONF_CH), F32)],
        scratch_shapes=[pltpu.VMEM((CONF_PAD + chunk, CONF_CH), F32),
                        pltpu.VMEM((SUBLANES - 1, CONF_PAD + chunk - SUBLANES, CONF_CH), F32)],
        compiler_params=_params(("arbitrary", "arbitrary"), 24), name="conf_conv")(u, st0, cw, cb, lg, lb)


def _pad_cols(w, width):
    return jnp.pad(w, ((0, 0), (0, width - w.shape[1])))


def _split_cols(w, sizes):
    out, off = [], 0
    for s in sizes:
        out.append(w[:, off:off + s])
        off += s
    return out


def _pad_t(a, bsz, t, t_pad):
    a = a.reshape(bsz, t, a.shape[-1])
    return a if t_pad == t else jnp.pad(a, ((0, 0), (0, t_pad - t), (0, 0)))


def _ceil_to(x, m):
    return -(-x // m) * m


def kernel(x_prompt, x_sample, cache_fox_k, cache_fox_v, cache_fox_logf, page_table, state_ssm, state_ssm_conv, state_gla, state_conf_conv, g_mix, g_mlp, g_final, w_in_ab, ssm_conv_w, ssm_conv_b, ssm_dt_bias, ssm_a_log, ssm_d, ssm_norm_g, gla_gate_w2, gla_gate_b, gla_norm_g, w_out_ab, w_in_cd, fox_b_f, conf_conv_w, conf_conv_b, conf_ln_g, conf_ln_b, w_out_cd, w_mlp_up, w_mlp_down):
    row = lambda v: v.reshape(1, -1)
    wz, wxbc, wdt, wq, wk, wv, wg, wlr = _split_cols(w_in_ab[0], AB_SPLIT)
    w_ab = jnp.concatenate([wz, wxbc, wq, wk, wv, wg, _pad_cols(wdt, LANES), _pad_cols(wlr, LANES)], axis=1).astype(BF16)
    cq, ck, cv, cf, cu = _split_cols(w_in_cd[0], CD_SPLIT)
    w_cd = jnp.concatenate([cq, ck, cv, cu, _pad_cols(cf, LANES)], axis=1).astype(BF16)
    w_oab = w_out_ab[0].astype(BF16)
    w_ocd = w_out_cd[0].astype(BF16)
    w_up = w_mlp_up.astype(BF16)
    w_dn = w_mlp_down.astype(BF16)
    dtb = _pad_cols(row(ssm_dt_bias[0]), LANES)
    alog = _pad_cols(row(ssm_a_log[0]), LANES)
    dsk = row(jnp.repeat(ssm_d[0], SSM_HEAD_DIM))
    e16 = (jnp.arange(LANES)[:, None] == jnp.arange(SSM_INNER)[None, :] // SSM_HEAD_DIM).astype(BF16)
    w2 = jnp.pad(gla_gate_w2[0], ((0, LANES - GLA_RANK), (0, 0))).astype(BF16)
    bfp = _pad_cols(row(fox_b_f[0]), LANES)
    width = FOX_HEADS * FOX_HEAD_DIM
    n_pool = cache_fox_k.shape[1]
    cache_k = jnp.transpose(cache_fox_k[0], (0, 2, 3, 1)).reshape(n_pool, width, PAGE_SIZE)
    cache_v = jnp.transpose(cache_fox_v[0], (0, 2, 3, 1)).reshape(n_pool, width, PAGE_SIZE)
    lft = jnp.swapaxes(cache_fox_logf[0], 1, 2)

    def trunk(x3, sample):
        bsz, T, _ = x3.shape
        n = bsz * T
        tm = min(n, 256)
        tm_big = min(n, 512)
        x = x3.reshape(n, D_MODEL)
        if sample:
            h0 = state_ssm[0].reshape(bsz, SSM_INNER, SSM_STATE)
            conv0, s0, conf0 = state_ssm_conv[0], state_gla[0].reshape(bsz, GLA_HEADS * GLA_DK, GLA_DV), state_conf_conv[0]
        else:
            h0 = jnp.zeros((bsz, SSM_INNER, SSM_STATE), F32)
            conv0 = jnp.zeros((bsz, SSM_CONV - 1, SSM_CONV_CH), F32)
            s0 = jnp.zeros((bsz, GLA_HEADS * GLA_DK, GLA_DV), F32)
            conf0 = jnp.zeros((bsz, CONF_WIDTH - 1, CONF_CH), F32)

        z, xbc, q, k, v, g, dtp, glr = _norm_proj(x, row(g_mix[0]), w_ab, AB_SECTIONS, tm)
        ta = _ceil_to(T, SSD_CHUNK)
        y, h_new, conv_new = _ssd(_pad_t(z, bsz, T, ta), _pad_t(xbc, bsz, T, ta), _pad_t(dtp, bsz, T, ta), conv0, h0,
                                  ssm_conv_w[0], row(ssm_conv_b[0]), dtb, alog, dsk, row(ssm_norm_g[0]), e16, T)
        tb = _ceil_to(T, GLA_STEP)
        o, s_new = _gla(_pad_t(q, bsz, T, tb), _pad_t(k, bsz, T, tb), _pad_t(v, bsz, T, tb), _pad_t(g, bsz, T, tb),
                        _pad_t(glr, bsz, T, tb), s0, w2, row(gla_gate_b[0]), row(gla_norm_g[0]), T)
        y = y[:, :T].reshape(n, SSM_INNER)
        o = o[:, :T].reshape(n, GLA_HEADS * GLA_DV)
        x = _out_proj(x, y, o, w_oab, tm_big)
        x = _mlp(x, row(g_mlp[0]), w_up[0], w_dn[0], row(g_final), tm_big, False)

        tg = _ceil_to(T, GATE_CHUNK)
        head_shape = (1, bsz, T, FOX_HEADS, FOX_HEAD_DIM)
        if sample:
            q, k, v, u, f = _norm_proj(x, row(g_mix[1]), w_cd, CD_SECTIONS, tm)
            lf_t, ct = _fox_gate(_pad_t(f, bsz, T, tg), bfp)
            q3, k3, v3 = (a.reshape(bsz, T, width) for a in (q, k, v))
            att = _fox_sample(page_table, q3, k3, v3, ct, cache_k, cache_v, lft)
            k_out, v_out = k.reshape(head_shape), v.reshape(head_shape)
        else:
            q, k, kt, vt, u, f = _norm_proj(x, row(g_mix[1]), w_cd, CD_SECTIONS, tm, ("n", "nt", "t", "n", "n"), T)
            lf_t, ct = _fox_gate(_pad_t(f, bsz, T, tg), bfp)
            att = _fox_prompt(q.reshape(bsz, T, width), k.reshape(bsz, T, width), vt, ct)
            to_rows = lambda a: jnp.transpose(a.reshape(1, bsz, FOX_HEADS, FOX_HEAD_DIM, T), (0, 1, 4, 2, 3))
            k_out, v_out = to_rows(kt), to_rows(vt)
        lf = jnp.swapaxes(lf_t, 1, 2)[:, :T]
        cmod, conf_new = _conf(u.reshape(bsz, T, 2 * CONF_CH), conf0, conf_conv_w[0], row(conf_conv_b[0]),
                               row(conf_ln_g[0]), row(conf_ln_b[0]), min(T, CONF_CHUNK))
        x = _out_proj(x, att.reshape(n, width), cmod.reshape(n, CONF_CH), w_ocd, tm_big)
        x = _mlp(x, row(g_mlp[1]), w_up[1], w_dn[1], row(g_final), tm_big, True)
        return (x.reshape(bsz, T, D_MODEL), h_new.reshape(1, bsz, SSM_HEADS, SSM_HEAD_DIM, SSM_STATE), conv_new[None],
                s_new.reshape(1, bsz, GLA_HEADS, GLA_DK, GLA_DV), k_out, v_out, lf[None], conf_new[None])

    yp, *rest_p = trunk(x_prompt, False)
    ys, *rest_s = trunk(x_sample, True)
    return (yp, ys, *rest_p, *rest_s)
```

```python
import functools

import jax
import jax.numpy as jnp
from jax import lax
from jax.experimental import pallas as pl
from jax.experimental.pallas import tpu as pltpu

F32 = jnp.float32
BF16 = jnp.bfloat16
EPS = 1e-6
LOG2E = 1.4426950408889634

LANES = 128
SUBLANES = 8
MIB = 1024 * 1024

D_MODEL = 1024
D_FF = 4 * D_MODEL
SSM_HEADS = 16
SSM_HEAD_DIM = 64
SSM_INNER = SSM_HEADS * SSM_HEAD_DIM
SSM_GROUPS = 2
SSM_STATE = 128
SSM_CONV = 4
SSM_CONV_CH = SSM_INNER + 2 * SSM_GROUPS * SSM_STATE
GLA_HEADS = 4
GLA_DK = 128
GLA_DV = 256
GLA_RANK = 16
GLA_GATE_NORM = 16.0
FOX_HEADS = 16
FOX_HEAD_DIM = 64
CONF_CH = 512
CONF_WIDTH = 31
PAGE_SIZE = 128

AB_SPLIT = (SSM_INNER, SSM_CONV_CH, SSM_HEADS, GLA_HEADS * GLA_DK, GLA_HEADS * GLA_DK,
            GLA_HEADS * GLA_DV, GLA_HEADS * GLA_DV, GLA_RANK)
CD_SPLIT = (FOX_HEADS * FOX_HEAD_DIM,) * 3 + (FOX_HEADS, 2 * CONF_CH)
AB_SECTIONS = (SSM_INNER, SSM_CONV_CH, GLA_HEADS * GLA_DK, GLA_HEADS * GLA_DK, GLA_HEADS * GLA_DV,
               GLA_HEADS * GLA_DV, LANES, LANES)
CD_SECTIONS = (1024, 1024, 1024, 2 * CONF_CH, LANES)

SSD_CHUNK = 128
GLA_CHUNK = 64
GLA_STEP = 128
SHORT_CHUNK = 16
GATE_CHUNK = 128
FOX_TQ = 256
FOX_TK = 256
FOX_SLOTS = 4
CONF_CHUNK = 256
CONF_PAD = 32
CONF_ROWS = 32
SSD_PAD = 8
FF_CHUNK = 1024
PAGES_PER_STEP = 16


def _dot(a, b):
    return jnp.dot(a, b, preferred_element_type=F32)


def _dot_nt(a, b):
    return lax.dot_general(a, b, (((1,), (1,)), ((), ())), preferred_element_type=F32)


def _dot_tn(a, b):
    return lax.dot_general(a, b, (((0,), (0,)), ((), ())), preferred_element_type=F32)


def _split3(x):
    hi = x.astype(BF16)
    r1 = x - hi.astype(F32)
    mid = r1.astype(BF16)
    lo = (r1 - mid.astype(F32)).astype(BF16)
    return hi, mid, lo


def _dot_exact_l(m, x):
    hi, mid, lo = _split3(x)
    return _dot(m, lo) + _dot(m, mid) + _dot(m, hi)


def _dot_exact_r(x, m):
    hi, mid, lo = _split3(x)
    return _dot(lo, m) + _dot(mid, m) + _dot(hi, m)


def _sigmoid(x):
    return 1.0 / (1.0 + jnp.exp(-x))


def _silu(x):
    return x * _sigmoid(x)


def _softplus(x):
    return jnp.maximum(x, 0.0) + jnp.log1p(jnp.exp(-jnp.abs(x)))


def _log_sigmoid(x):
    return jnp.minimum(x, 0.0) - jnp.log1p(jnp.exp(-jnp.abs(x)))


def _rms(x, g):
    return x * lax.rsqrt(jnp.mean(x * x, axis=-1, keepdims=True) + EPS) * g


def _iota(shape, dim):
    return lax.broadcasted_iota(jnp.int32, shape, dim)


def _tril(n):
    return _iota((n, n), 1) <= _iota((n, n), 0)


def _params(semantics, vmem_mib):
    return pltpu.CompilerParams(dimension_semantics=semantics, vmem_limit_bytes=vmem_mib * MIB)


def _resident(shape):
    nd = len(shape)
    return pl.BlockSpec(shape, lambda *_: (0,) * nd, pipeline_mode=pl.Buffered(1))


def _norm_proj_kernel(x_ref, g_ref, w_ref, *out_refs, sections):
    xb = _rms(x_ref[...], g_ref[...]).astype(BF16)
    outs = iter(out_refs)
    for off, width, mode in sections:
        y = _dot(xb, w_ref[:, off:off + width])
        if "n" in mode:
            next(outs)[...] = y
        if "t" in mode:
            next(outs)[...] = y.T


def _norm_proj(x, g, w, widths, tm, modes=None, seq=None):
    n = x.shape[0]
    modes = modes or ("n",) * len(widths)
    offs = [sum(widths[:i]) for i in range(len(widths))]
    kern = functools.partial(_norm_proj_kernel, sections=tuple(zip(offs, widths, modes)))
    wtot = w.shape[1]
    out_specs, out_shape = [], []
    for s, mode in zip(widths, modes):
        if "n" in mode:
            out_specs.append(pl.BlockSpec((tm, s), lambda i: (i, 0)))
            out_shape.append(jax.ShapeDtypeStruct((n, s), F32))
        if "t" in mode:
            per_seq = seq // tm
            out_specs.append(pl.BlockSpec((None, s, tm), lambda i: (i // per_seq, 0, i % per_seq)))
            out_shape.append(jax.ShapeDtypeStruct((n // seq, s, seq), F32))
    n_out = sum(s * len(mode) for s, mode in zip(widths, modes))
    vmem = (2 * tm * D_MODEL * 4 + D_MODEL * wtot * 2 + 3 * tm * n_out * 4) // MIB + 4
    return pl.pallas_call(
        kern, grid=(n // tm,),
        in_specs=[pl.BlockSpec((tm, D_MODEL), lambda i: (i, 0)), _resident((1, D_MODEL)), _resident(w.shape)],
        out_specs=out_specs, out_shape=out_shape,
        compiler_params=_params(("arbitrary",), vmem), name="norm_proj")(x, g, w)


def _ssd_kernel(z_ref, xbc_ref, dt_ref, conv0_ref, h0_ref, cw_ref, cb_ref, dtb_ref, alog_ref, dsk_ref, ng_ref,
                e16_ref, y_ref, hout_ref, convout_ref, xbuf, h_scr, y_scr, *, chunk, n_chunks, t_valid):
    c = pl.program_id(1)
    L = chunk
    P0 = SSD_PAD - (SSM_CONV - 1)
    last_valid = t_valid - (n_chunks - 1) * L

    @pl.when(c == 0)
    def _():
        xbuf[P0:SSD_PAD, :] = conv0_ref[...]
        h_scr[...] = h0_ref[...]

    xbuf[SSD_PAD:SSD_PAD + L, :] = xbc_ref[...]
    cw = cw_ref[...]
    conv = cb_ref[...] + xbuf[P0:P0 + L, :] * cw[0:1, :]
    for j in range(1, SSM_CONV):
        conv = conv + xbuf[P0 + j:P0 + j + L, :] * cw[j:j + 1, :]
    act = _silu(conv)
    xs = act[:, :SSM_INNER]
    bm_b = act[:, SSM_INNER:SSM_INNER + SSM_GROUPS * SSM_STATE].astype(BF16)
    cm_b = act[:, SSM_INNER + SSM_GROUPS * SSM_STATE:].astype(BF16)

    row = _iota((L, LANES), 0)
    lane = _iota((L, LANES), 1)
    live = (lane < SSM_HEADS) & (c * L + row < t_valid)
    dt = jnp.where(live, _softplus(dt_ref[...] + dtb_ref[...]), 0.0)
    a = -jnp.exp(alog_ref[...]) * dt
    causal = _tril(L)
    tri = jnp.where(causal, 1.0, 0.0).astype(BF16)
    ac = _dot_exact_l(tri, a)
    e16 = e16_ref[...]
    ac_x = _dot_exact_r(ac, e16)
    dt_x = _dot_exact_r(dt, e16)
    if L == LANES:
        ac_t = ac.T
    else:
        ac_t = jnp.concatenate([ac, jnp.zeros((LANES - L, LANES), F32)], axis=0).T[:, :L]
    ac_last = ac[L - 1:L, :]
    ac_last_x = ac_x[L - 1:L, :]
    to_end_x = jnp.exp(ac_last_x - ac_x)
    eac_x = jnp.exp(ac_x)
    xdt = xs * dt_x
    xdt_b = xdt.astype(BF16)
    xend_b = (xdt * to_end_x).astype(BF16)
    lane_lo = lane < SSM_HEAD_DIM
    row_lo = _iota((LANES, LANES), 0) < SSM_HEAD_DIM
    heads_per_group = SSM_HEADS // SSM_GROUPS
    for g in range(SSM_GROUPS):
        bg = bm_b[:, g * SSM_STATE:(g + 1) * SSM_STATE]
        cg = cm_b[:, g * SSM_STATE:(g + 1) * SSM_STATE]
        cb = _dot_nt(cg, bg)
        for p in range(g * heads_per_group // 2, (g + 1) * heads_per_group // 2):
            sl = slice(p * LANES, (p + 1) * LANES)
            ys = []
            for j in range(2):
                h = 2 * p + j
                seg = ac[:, h:h + 1] - ac_t[h:h + 1, :]
                m = (cb * jnp.exp(jnp.where(causal, seg, -jnp.inf))).astype(BF16)
                ys.append(_dot(m, xdt_b[:, sl]))
            y_diag = jnp.where(lane_lo, ys[0], ys[1])
            s_old = h_scr[sl, :]
            y_off = _dot_nt(cg, s_old.astype(BF16)) * eac_x[:, sl]
            dec = jnp.exp(jnp.where(row_lo, ac_last[:, 2 * p:2 * p + 1], ac_last[:, 2 * p + 1:2 * p + 2]))
            h_scr[sl, :] = s_old * dec + _dot_tn(xend_b[:, sl], bg)
            y_scr[:, sl] = y_diag + y_off + dsk_ref[:, sl] * xs[:, sl]

    y_ref[...] = _rms(y_scr[...] * _silu(z_ref[...]), ng_ref[...])

    @pl.when(c == n_chunks - 1)
    def _():
        hout_ref[...] = h_scr[...]
        convout_ref[...] = xbuf[P0 + last_valid:SSD_PAD + last_valid, :]

    tail = xbuf[P0 + L:SSD_PAD + L, :]
    xbuf[P0:SSD_PAD, :] = tail


def _ssd(z, xbc, dtp, conv0, h0, cw, cb, dtb, alog, dsk, ng, e16, t_valid, chunk):
    bsz, t_pad, _ = z.shape
    L = chunk
    nc = t_pad // L
    kern = functools.partial(_ssd_kernel, chunk=chunk, n_chunks=nc, t_valid=t_valid)
    tok = lambda w: pl.BlockSpec((None, L, w), lambda b, c: (b, c, 0))
    per_b = lambda s: pl.BlockSpec((None,) + s, lambda b, c: (b, 0, 0))
    return pl.pallas_call(
        kern, grid=(bsz, nc),
        in_specs=[tok(SSM_INNER), tok(SSM_CONV_CH), tok(LANES), per_b((SSM_CONV - 1, SSM_CONV_CH)),
                  per_b((SSM_INNER, SSM_STATE)), _resident(cw.shape), _resident(cb.shape), _resident(dtb.shape),
                  _resident(alog.shape), _resident(dsk.shape), _resident(ng.shape), _resident(e16.shape)],
        out_specs=[tok(SSM_INNER), per_b((SSM_INNER, SSM_STATE)), per_b((SSM_CONV - 1, SSM_CONV_CH))],
        out_shape=[jax.ShapeDtypeStruct((bsz, t_pad, SSM_INNER), F32),
                   jax.ShapeDtypeStruct((bsz, SSM_INNER, SSM_STATE), F32),
                   jax.ShapeDtypeStruct((bsz, SSM_CONV - 1, SSM_CONV_CH), F32)],
        scratch_shapes=[pltpu.VMEM((SSD_PAD + L, SSM_CONV_CH), F32), pltpu.VMEM((SSM_INNER, SSM_STATE), F32),
                        pltpu.VMEM((L, SSM_INNER), F32)],
        compiler_params=_params(("arbitrary", "arbitrary"), 40), name="ssd")(
            z, xbc, dtp, conv0, h0, cw, cb, dtb, alog, dsk, ng, e16)


def _gla_kernel(q_ref, k_ref, v_ref, g_ref, glr_ref, s0_ref, w2_ref, gb_ref, ng_ref, o_ref, sout_ref, s_scr,
                *, step_rows, chunk, n_chunks, t_valid):
    c = pl.program_id(1)
    R, L = step_rows, chunk
    n_sub = R // L
    chunk_shift = L.bit_length() - 1
    width = GLA_HEADS * GLA_DK

    @pl.when(c == 0)
    def _():
        s_scr[...] = s0_ref[...]

    x = _dot(glr_ref[...].astype(BF16), w2_ref[...]) + gb_ref[...]
    logf = _log_sigmoid(x) * (1.0 / GLA_GATE_NORM)
    logf = jnp.where(c * R + _iota((R, width), 0) < t_valid, logf, 0.0)
    row, col = _iota((R, R), 0), _iota((R, R), 1)
    causal = ((row >> chunk_shift) == (col >> chunk_shift)) & (col <= row)
    tri = jnp.where(causal, 1.0, 0.0).astype(BF16)
    bcum = _dot_exact_l(tri, logf)
    bl = [bcum[(s + 1) * L - 1:(s + 1) * L, :] for s in range(n_sub)]
    bl_rows = jnp.concatenate([jnp.broadcast_to(b, (L, width)) for b in bl], axis=0)
    q_dec = q_ref[...] * (GLA_DK ** -0.5) * jnp.exp(bcum)
    k = k_ref[...]
    k_inv = k * jnp.exp(-bcum)
    k_end = k * jnp.exp(bl_rows - bcum)
    for h in range(GLA_HEADS):
        ks = slice(h * GLA_DK, (h + 1) * GLA_DK)
        vs = slice(h * GLA_DV, (h + 1) * GLA_DV)
        qd = q_dec[:, ks].astype(BF16)
        ke = k_end[:, ks].astype(BF16)
        vb = v_ref[:, vs].astype(BF16)
        att = jnp.where(causal, _dot_nt(qd, k_inv[:, ks].astype(BF16)), 0.0)
        o = _dot(att.astype(BF16), vb)
        state = s_scr[ks, :]
        inter = []
        for s in range(n_sub):
            rs = slice(s * L, (s + 1) * L)
            inter.append(_dot(qd[rs], state.astype(BF16)))
            dcol = jnp.broadcast_to(jnp.exp(bl[s][:, ks]), (GLA_DK, GLA_DK)).T
            state = state * jnp.concatenate([dcol, dcol], axis=1) + _dot_tn(ke[rs], vb[rs])
        s_scr[ks, :] = state
        o = o + jnp.concatenate(inter, axis=0)
        o_ref[:, vs] = _rms(o, ng_ref[...]) * _silu(g_ref[:, vs])

    @pl.when(c == n_chunks - 1)
    def _():
        sout_ref[...] = s_scr[...]


def _gla(q, k, v, g, glr, s0, w2, gb, ng, t_valid, step_rows, chunk):
    bsz, t_pad, _ = q.shape
    L = step_rows
    nc = t_pad // L
    kern = functools.partial(_gla_kernel, step_rows=step_rows, chunk=chunk, n_chunks=nc, t_valid=t_valid)
    tok = lambda w: pl.BlockSpec((None, L, w), lambda b, c: (b, c, 0))
    st = pl.BlockSpec((None, GLA_HEADS * GLA_DK, GLA_DV), lambda b, c: (b, 0, 0))
    return pl.pallas_call(
        kern, grid=(bsz, nc),
        in_specs=[tok(GLA_HEADS * GLA_DK), tok(GLA_HEADS * GLA_DK), tok(GLA_HEADS * GLA_DV), tok(GLA_HEADS * GLA_DV),
                  tok(LANES), st, _resident(w2.shape), _resident(gb.shape), _resident(ng.shape)],
        out_specs=[tok(GLA_HEADS * GLA_DV), st],
        out_shape=[jax.ShapeDtypeStruct((bsz, t_pad, GLA_HEADS * GLA_DV), F32),
                   jax.ShapeDtypeStruct((bsz, GLA_HEADS * GLA_DK, GLA_DV), F32)],
        scratch_shapes=[pltpu.VMEM((GLA_HEADS * GLA_DK, GLA_DV), F32)],
        compiler_params=_params(("arbitrary", "arbitrary"), 32), name="gla")(q, k, v, g, glr, s0, w2, gb, ng)


def _proj_mlp_kernel(x_ref, a_ref, b_ref, wo_ref, g_ref, wu_ref, wd_ref, gf_ref, o_ref, *, wa, final_norm):
    x = (x_ref[...] + _dot(a_ref[...].astype(BF16), wo_ref[:wa, :])
         + _dot(b_ref[...].astype(BF16), wo_ref[wa:, :]))
    o_ref[...] = x
    xb = _rms(o_ref[...], g_ref[...]).astype(BF16)
    for c in range(D_FF // FF_CHUNK):
        sl = slice(c * FF_CHUNK, (c + 1) * FF_CHUNK)
        h = jnp.maximum(_dot(xb, wu_ref[:, sl]), 0.0)
        o_ref[...] += _dot((h * h).astype(BF16), wd_ref[sl, :])
    if final_norm:
        o_ref[...] = _rms(o_ref[...], gf_ref[...])


def _proj_mlp(x, a, b, wo, g, wu, wd, gf, tm, final_norm):
    n = x.shape[0]
    wa, wb = a.shape[1], b.shape[1]
    row = lambda s: pl.BlockSpec((tm, s), lambda i: (i, 0))
    weights = (wo.size + wu.size + wd.size) * 2
    tiles = 2 * tm * (2 * D_MODEL + wa + wb) * 4 + tm * (2 * D_MODEL + 2 * FF_CHUNK) * 4
    return pl.pallas_call(
        functools.partial(_proj_mlp_kernel, wa=wa, final_norm=final_norm), grid=(n // tm,),
        in_specs=[row(D_MODEL), row(wa), row(wb), _resident(wo.shape), _resident(g.shape), _resident(wu.shape),
                  _resident(wd.shape), _resident(gf.shape)],
        out_specs=row(D_MODEL), out_shape=jax.ShapeDtypeStruct((n, D_MODEL), F32),
        compiler_params=_params(("arbitrary",), (weights + tiles) // MIB + 6), name="proj_mlp")(
            x, a, b, wo, g, wu, wd, gf)


def _fox_gate_kernel(f_ref, bf_ref, lft_ref, ct_ref, carry):
    @pl.when(pl.program_id(1) == 0)
    def _():
        carry[...] = jnp.zeros_like(carry)

    L = GATE_CHUNK
    lf = _log_sigmoid(f_ref[...] + bf_ref[...])
    tri = jnp.where(_tril(L), 1.0, 0.0).astype(BF16)
    cblk = carry[...] + _dot_exact_l(tri, lf)
    carry[...] = cblk[L - 1:L, :]
    lft_ref[...] = lf.T[:FOX_HEADS, :]
    ct_ref[...] = cblk.T[:FOX_HEADS, :]


def _fox_gate(f, bf):
    bsz, t_pad, _ = f.shape
    L = GATE_CHUNK
    return pl.pallas_call(
        _fox_gate_kernel, grid=(bsz, t_pad // L),
        in_specs=[pl.BlockSpec((None, L, LANES), lambda b, t: (b, t, 0)), _resident(bf.shape)],
        out_specs=[pl.BlockSpec((None, FOX_HEADS, L), lambda b, t: (b, 0, t)),
                   pl.BlockSpec((None, FOX_HEADS, L), lambda b, t: (b, 0, t))],
        out_shape=[jax.ShapeDtypeStruct((bsz, FOX_HEADS, t_pad), F32),
                   jax.ShapeDtypeStruct((bsz, FOX_HEADS, t_pad), F32)],
        scratch_shapes=[pltpu.VMEM((1, LANES), F32)],
        compiler_params=_params(("arbitrary", "arbitrary"), 16), name="fox_gate")(f, bf)


def _fox_prompt_kernel(q_ref, k_ref, vt_ref, ct_ref, o_ref, kb_scr, vtb_scr, ck_scr, *score_scr, n_q_blocks):
    hp = pl.program_id(1)
    tq, tk = FOX_TQ, FOX_TK
    dh = FOX_HEAD_DIM
    heads = [2 * hp, 2 * hp + 1]

    kb_scr[...] = k_ref[...].astype(BF16)
    vtb_scr[...] = vt_ref[...].astype(BF16)
    for j in range(2):
        c_row = ct_ref[pl.ds(heads[j], 1), :] * LOG2E
        ck_scr[j] = jnp.broadcast_to(c_row, (LANES, c_row.shape[1])).T

    lane = _iota((tq, LANES), 1)
    causal = _iota((tk, tq), 0) <= _iota((tk, tq), 1)

    def fold8(x, op):
        parts = [x[r * 8:(r + 1) * 8, :] for r in range(x.shape[0] // 8)]
        while len(parts) > 1:
            parts = [op(parts[i], parts[i + 1]) for i in range(0, len(parts), 2)]
        return parts[0]

    st = [dict() for _ in range(n_q_blocks)]
    bufs = lambda v, j: (score_scr[2 * (v % 2) + j], score_scr[FOX_SLOTS + 2 * (v % 2) + j])

    def p1_block(v, ki):
        if ki == 0:
            q = q_ref[v * tq:(v + 1) * tq, :] * (dh ** -0.5 * LOG2E)
            st[v]["qm"] = [jnp.where(lane < dh, q, 0.0).astype(BF16), jnp.where(lane >= dh, q, 0.0).astype(BF16)]
            st[v]["mx"] = [jnp.full((8, tq), -jnp.inf, F32)] * 2
        rows = slice(ki * tk, (ki + 1) * tk)
        for j in range(2):
            s = _dot_nt(kb_scr[rows, :], st[v]["qm"][j]) - jnp.concatenate([ck_scr[j, rows, :]] * (tq // LANES), axis=1)
            if ki == v:
                s = jnp.where(causal, s, -jnp.inf)
            bufs(v, j)[0][rows, :] = s
            st[v]["mx"][j] = jnp.maximum(st[v]["mx"][j], fold8(s, jnp.maximum))

    def p1_finish(v):
        cq = [ct_ref[pl.ds(heads[j], 1), v * tq:(v + 1) * tq] * LOG2E for j in range(2)]
        m_new = [jnp.max(st[v]["mx"][j], axis=0, keepdims=True) + cq[j] for j in range(2)]
        st[v]["shift"] = [m_new[j] - cq[j] for j in range(2)]
        st[v]["lsum"] = [jnp.zeros((8, tq), F32)] * 2

    def p2_block(v, ki):
        rows = slice(ki * tk, (ki + 1) * tk)
        for j in range(2):
            s_scr, p_scr = bufs(v, j)
            p = jnp.exp2(s_scr[rows, :] - st[v]["shift"][j])
            st[v]["lsum"][j] = st[v]["lsum"][j] + fold8(p, jnp.add)
            p_scr[rows, :] = p.astype(BF16)

    def p2_finish(v):
        n_k = (v + 1) * tk
        outs = []
        for j in range(2):
            acc = _dot(vtb_scr[j * dh:(j + 1) * dh, 0:n_k], bufs(v, j)[1][0:n_k, :])
            outs.append(acc / jnp.sum(st[v]["lsum"][j], axis=0, keepdims=True))
        o_ref[v * tq:(v + 1) * tq, :] = jnp.concatenate(outs, axis=0).T

    def stage(block, finish, v):
        return [functools.partial(block, v, ki) for ki in range(v + 1)] + [functools.partial(finish, v)]

    for step in stage(p1_block, p1_finish, 0):
        step()
    for v in range(n_q_blocks):
        ahead = stage(p1_block, p1_finish, v + 1) if v + 1 < n_q_blocks else []
        behind = stage(p2_block, p2_finish, v)
        for i in range(max(len(ahead), len(behind))):
            for steps in (ahead, behind):
                if i < len(steps):
                    steps[i]()


def _fox_prompt(q, k, vt, ct):
    bsz, T, _ = q.shape
    tq = FOX_TQ
    return pl.pallas_call(
        functools.partial(_fox_prompt_kernel, n_q_blocks=T // tq), grid=(bsz, FOX_HEADS // 2),
        in_specs=[pl.BlockSpec((None, T, LANES), lambda b, hp: (b, 0, hp)),
                  pl.BlockSpec((None, T, LANES), lambda b, hp: (b, 0, hp)),
                  pl.BlockSpec((None, LANES, T), lambda b, hp: (b, hp, 0)),
                  pl.BlockSpec((None, FOX_HEADS, T), lambda b, hp: (b, 0, 0))],
        out_specs=pl.BlockSpec((None, T, LANES), lambda b, hp: (b, 0, hp)),
        out_shape=jax.ShapeDtypeStruct((bsz, T, FOX_HEADS * FOX_HEAD_DIM), F32),
        scratch_shapes=([pltpu.VMEM((T, LANES), BF16), pltpu.VMEM((LANES, T), BF16), pltpu.VMEM((2, T, LANES), F32)]
                        + [pltpu.VMEM((T, tq), F32)] * FOX_SLOTS + [pltpu.VMEM((T, tq), BF16)] * FOX_SLOTS),
        compiler_params=_params(("arbitrary",) * 2, 40), name="fox_prompt")(q, k, vt, ct)


def _fox_sample_kernel(pt_ref, q_ref, kn_ref, vn_ref, cnt_ref, *refs, n_steps):
    G = PAGES_PER_STEP
    k_refs, v_refs, lf_refs = refs[:G], refs[G:2 * G], refs[2 * G:3 * G]
    o_ref, qbd, kt_scr, vt_scr, m_scr, l_scr, acc_scr, carry, cncol = refs[3 * G:]
    p = pl.program_id(1)
    R = LANES
    T = R // FOX_HEADS
    width = FOX_HEADS * FOX_HEAD_DIM
    dh_shift = FOX_HEAD_DIM.bit_length() - 1
    t_shift = T.bit_length() - 1
    row = _iota((R, LANES), 0)
    lane = _iota((R, LANES), 1)

    def expand_heads(xt):
        return jnp.concatenate([jnp.broadcast_to(xt[h:h + 1, :], (T, xt.shape[1])) for h in range(FOX_HEADS)], axis=0)

    def update(s, vb):
        m = m_scr[...]
        cq = cncol[...]
        m_new = jnp.maximum(m, jnp.max(s, axis=-1, keepdims=True) + cq)
        pr = jnp.exp(s - (m_new - cq))
        alpha = jnp.exp(m - m_new)
        m_scr[...] = m_new
        l_scr[...] = alpha * l_scr[...] + jnp.sum(pr, axis=-1, keepdims=True)
        acc_scr[...] = alpha * acc_scr[...] + _dot(pr.astype(BF16), vb)

    @pl.when(p == 0)
    def _():
        q = q_ref[...] * (FOX_HEAD_DIM ** -0.5)
        qt = jnp.concatenate([q] * FOX_HEADS, axis=0)
        own = (_iota((R, width), 1) >> dh_shift) == (_iota((R, width), 0) >> t_shift)
        qbd[...] = jnp.where(own, qt, 0.0).astype(BF16)
        m_scr[...] = jnp.full_like(m_scr, -jnp.inf)
        l_scr[...] = jnp.zeros_like(l_scr)
        acc_scr[...] = jnp.zeros_like(acc_scr)
        carry[...] = jnp.zeros_like(carry)
        cn = expand_heads(cnt_ref[...])
        t_of_row = row & (T - 1)
        cncol[...] = jnp.sum(jnp.where(lane == t_of_row, cn, 0.0), axis=-1, keepdims=True)
        pad = jnp.zeros((LANES - T, width), F32)
        kb = jnp.concatenate([kn_ref[...], pad], axis=0).astype(BF16)
        vb = jnp.concatenate([vn_ref[...], pad], axis=0).astype(BF16)
        s = _dot_nt(qbd[...], kb) - cn
        update(jnp.where(lane <= t_of_row, s, -jnp.inf), vb)

    for i in range(G):
        kt_scr[:, i * PAGE_SIZE:(i + 1) * PAGE_SIZE] = k_refs[i][...].astype(BF16)
        vt_scr[:, i * PAGE_SIZE:(i + 1) * PAGE_SIZE] = v_refs[i][...].astype(BF16)
    later = jnp.where(_iota((LANES, LANES), 0) > _iota((LANES, LANES), 1), 1.0, 0.0).astype(BF16)
    lf_all = jnp.concatenate([lf_refs[i][...] for i in range(G)], axis=0)
    within = _dot_exact_r(lf_all, later)
    total = jnp.sum(lf_all, axis=-1, keepdims=True)
    run = carry[...]
    sufs = []
    for i in range(G):
        sufs.append(run + within[i * FOX_HEADS:(i + 1) * FOX_HEADS, :])
        run = run + total[i * FOX_HEADS:(i + 1) * FOX_HEADS, :]
    carry[...] = run
    s = _dot(qbd[...], kt_scr[...]) + expand_heads(jnp.concatenate(sufs, axis=1))
    m = m_scr[...]
    cq = cncol[...]
    m_new = jnp.maximum(m, jnp.max(s, axis=-1, keepdims=True) + cq)
    pr = jnp.exp(s - (m_new - cq))
    alpha = jnp.exp(m - m_new)
    m_scr[...] = m_new
    l_scr[...] = alpha * l_scr[...] + jnp.sum(pr, axis=-1, keepdims=True)
    acc_scr[...] = alpha * acc_scr[...] + _dot_nt(pr.astype(BF16), vt_scr[...])

    @pl.when(p == n_steps - 1)
    def _():
        an = acc_scr[...] / l_scr[...]
        out = jnp.zeros((T, width), F32)
        col_head = _iota((T, width), 1) >> dh_shift
        for h in range(FOX_HEADS):
            out = jnp.where(col_head == h, an[h * T:(h + 1) * T, :], out)
        o_ref[...] = out


def _fox_sample(page_table, q, kn, vn, cnt, cache_k, cache_v, lft):
    bsz, T, width = q.shape
    n_pages = page_table.shape[1]
    G = PAGES_PER_STEP
    n_steps = n_pages // G

    def page(i):
        return lambda b, p, pt: (pt[b, n_pages - 1 - (p * G + i)], 0, 0)

    per_b = lambda s: pl.BlockSpec((None,) + s, lambda b, p, pt: (b, 0, 0))
    in_specs = ([per_b((T, width)), per_b((T, width)), per_b((T, width)), per_b((FOX_HEADS, LANES))]
                + [pl.BlockSpec((None, width, PAGE_SIZE), page(i)) for i in range(G)]
                + [pl.BlockSpec((None, width, PAGE_SIZE), page(i)) for i in range(G)]
                + [pl.BlockSpec((None, FOX_HEADS, PAGE_SIZE), page(i)) for i in range(G)])
    grid_spec = pltpu.PrefetchScalarGridSpec(
        num_scalar_prefetch=1, grid=(bsz, n_steps), in_specs=in_specs, out_specs=per_b((T, width)),
        scratch_shapes=[pltpu.VMEM((LANES, width), BF16), pltpu.VMEM((width, G * PAGE_SIZE), BF16),
                        pltpu.VMEM((width, G * PAGE_SIZE), BF16), pltpu.VMEM((LANES, 1), F32),
                        pltpu.VMEM((LANES, 1), F32), pltpu.VMEM((LANES, width), F32), pltpu.VMEM((FOX_HEADS, 1), F32),
                        pltpu.VMEM((LANES, 1), F32)])
    return pl.pallas_call(
        functools.partial(_fox_sample_kernel, n_steps=n_steps), grid_spec=grid_spec,
        out_shape=jax.ShapeDtypeStruct((bsz, T, width), F32),
        compiler_params=_params(("arbitrary", "arbitrary"), 56), name="fox_sample")(
            page_table, q, kn, vn, cnt, *([cache_k] * G), *([cache_v] * G), *([lft] * G))


def _conf_kernel(u_ref, st0_ref, cw_ref, cb_ref, lg_ref, lb_ref, c_ref, stout_ref, buf, sh, *, chunk, n_chunks):
    t = pl.program_id(1)
    L = chunk
    P0 = CONF_PAD - (CONF_WIDTH - 1)

    @pl.when(t == 0)
    def _():
        buf[P0:CONF_PAD, :] = st0_ref[...]

    u = u_ref[...]
    buf[CONF_PAD:CONF_PAD + L, :] = u[:, :CONF_CH] * _sigmoid(u[:, CONF_CH:])
    span = L + CONF_PAD - SUBLANES
    for b in range(1, SUBLANES):
        sh[b - 1, 0:span, :] = buf[b:b + span, :]
    cw = cw_ref[...]
    rb = min(L, CONF_ROWS)
    for r0 in range(0, L, rb):
        acc = jnp.broadcast_to(cb_ref[...], (rb, CONF_CH))
        for j in range(CONF_WIDTH):
            a, b = divmod(P0 + j, SUBLANES)
            lo = r0 + a * SUBLANES
            src = buf[lo:lo + rb, :] if b == 0 else sh[b - 1, lo:lo + rb, :]
            acc = acc + src * cw[j:j + 1, :]
        xc = acc - jnp.mean(acc, axis=-1, keepdims=True)
        var = jnp.mean(xc * xc, axis=-1, keepdims=True)
        c_ref[r0:r0 + rb, :] = _silu(xc * lax.rsqrt(var + EPS) * lg_ref[...] + lb_ref[...])

    @pl.when(t == n_chunks - 1)
    def _():
        stout_ref[...] = buf[P0 + L:CONF_PAD + L, :]

    tail = buf[L:L + CONF_PAD, :]
    buf[0:CONF_PAD, :] = tail


def _conf(u, st0, cw, cb, lg, lb, chunk):
    bsz, T, _ = u.shape
    nc = T // chunk
    st = pl.BlockSpec((None, CONF_WIDTH - 1, CONF_CH), lambda b, t: (b, 0, 0))
    return pl.pallas_call(
        functools.partial(_conf_kernel, chunk=chunk, n_chunks=nc), grid=(bsz, nc),
        in_specs=[pl.BlockSpec((None, chunk, 2 * CONF_CH), lambda b, t: (b, t, 0)), st, _resident(cw.shape),
                  _resident(cb.shape), _resident(lg.shape), _resident(lb.shape)],
        out_specs=[pl.BlockSpec((None, chunk, CONF_CH), lambda b, t: (b, t, 0)), st],
        out_shape=[jax.ShapeDtypeStruct((bsz, T, CONF_CH), F32),
                   jax.ShapeDtypeStruct((bsz, CONF_WIDTH - 1, CONF_CH), F32)],
        scratch_shapes=[pltpu.VMEM((CONF_PAD + chunk, CONF_CH), F32),
                        pltpu.VMEM((SUBLANES - 1, CONF_PAD + chunk - SUBLANES, CONF_CH), F32)],
        compiler_params=_params(("arbitrary", "arbitrary"), 24), name="conf_conv")(u, st0, cw, cb, lg, lb)


def _pad_cols(w, width):
    return jnp.pad(w, ((0, 0), (0, width - w.shape[1])))


def _split_cols(w, sizes):
    out, off = [], 0
    for s in sizes:
        out.append(w[:, off:off + s])
        off += s
    return out


def _pad_t(a, bsz, t, t_pad):
    a = a.reshape(bsz, t, a.shape[-1])
    return a if t_pad == t else jnp.pad(a, ((0, 0), (0, t_pad - t), (0, 0)))


def _ceil_to(x, m):
    return -(-x // m) * m


def kernel(x_prompt, x_sample, cache_fox_k, cache_fox_v, cache_fox_logf, page_table, state_ssm, state_ssm_conv, state_gla, state_conf_conv, g_mix, g_mlp, g_final, w_in_ab, ssm_conv_w, ssm_conv_b, ssm_dt_bias, ssm_a_log, ssm_d, ssm_norm_g, gla_gate_w2, gla_gate_b, gla_norm_g, w_out_ab, w_in_cd, fox_b_f, conf_conv_w, conf_conv_b, conf_ln_g, conf_ln_b, w_out_cd, w_mlp_up, w_mlp_down):
    row = lambda v: v.reshape(1, -1)
    wz, wxbc, wdt, wq, wk, wv, wg, wlr = _split_cols(w_in_ab[0], AB_SPLIT)
    w_ab = jnp.concatenate([wz, wxbc, wq, wk, wv, wg, _pad_cols(wdt, LANES), _pad_cols(wlr, LANES)], axis=1).astype(BF16)
    cq, ck, cv, cf, cu = _split_cols(w_in_cd[0], CD_SPLIT)
    w_cd = jnp.concatenate([cq, ck, cv, cu, _pad_cols(cf, LANES)], axis=1).astype(BF16)
    w_oab = w_out_ab[0].astype(BF16)
    w_ocd = w_out_cd[0].astype(BF16)
    w_up = w_mlp_up.astype(BF16)
    w_dn = w_mlp_down.astype(BF16)
    dtb = _pad_cols(row(ssm_dt_bias[0]), LANES)
    alog = _pad_cols(row(ssm_a_log[0]), LANES)
    dsk = row(jnp.repeat(ssm_d[0], SSM_HEAD_DIM))
    e16 = (jnp.arange(LANES)[:, None] == jnp.arange(SSM_INNER)[None, :] // SSM_HEAD_DIM).astype(BF16)
    w2 = jnp.pad(gla_gate_w2[0], ((0, LANES - GLA_RANK), (0, 0))).astype(BF16)
    bfp = _pad_cols(row(fox_b_f[0]), LANES)
    width = FOX_HEADS * FOX_HEAD_DIM
    n_pool = cache_fox_k.shape[1]
    cache_k = jnp.transpose(cache_fox_k[0], (0, 2, 3, 1)).reshape(n_pool, width, PAGE_SIZE)
    cache_v = jnp.transpose(cache_fox_v[0], (0, 2, 3, 1)).reshape(n_pool, width, PAGE_SIZE)
    lft = jnp.swapaxes(cache_fox_logf[0], 1, 2)

    def trunk(x3, sample):
        bsz, T, _ = x3.shape
        n = bsz * T
        tm = min(n, 256)
        tm_big = min(n, 512)
        x = x3.reshape(n, D_MODEL)
        if sample:
            h0 = state_ssm[0].reshape(bsz, SSM_INNER, SSM_STATE)
            conv0, s0, conf0 = state_ssm_conv[0], state_gla[0].reshape(bsz, GLA_HEADS * GLA_DK, GLA_DV), state_conf_conv[0]
        else:
            h0 = jnp.zeros((bsz, SSM_INNER, SSM_STATE), F32)
            conv0 = jnp.zeros((bsz, SSM_CONV - 1, SSM_CONV_CH), F32)
            s0 = jnp.zeros((bsz, GLA_HEADS * GLA_DK, GLA_DV), F32)
            conf0 = jnp.zeros((bsz, CONF_WIDTH - 1, CONF_CH), F32)

        z, xbc, q, k, v, g, dtp, glr = _norm_proj(x, row(g_mix[0]), w_ab, AB_SECTIONS, tm)
        short = T <= SHORT_CHUNK
        ssd_chunk = SHORT_CHUNK if short else SSD_CHUNK
        gla_rows, gla_chunk = (SHORT_CHUNK, SHORT_CHUNK) if short else (GLA_STEP, GLA_CHUNK)
        ta = _ceil_to(T, ssd_chunk)
        y, h_new, conv_new = _ssd(_pad_t(z, bsz, T, ta), _pad_t(xbc, bsz, T, ta), _pad_t(dtp, bsz, T, ta), conv0, h0,
                                  ssm_conv_w[0], row(ssm_conv_b[0]), dtb, alog, dsk, row(ssm_norm_g[0]), e16, T,
                                  ssd_chunk)
        tb = _ceil_to(T, gla_rows)
        o, s_new = _gla(_pad_t(q, bsz, T, tb), _pad_t(k, bsz, T, tb), _pad_t(v, bsz, T, tb), _pad_t(g, bsz, T, tb),
                        _pad_t(glr, bsz, T, tb), s0, w2, row(gla_gate_b[0]), row(gla_norm_g[0]), T, gla_rows,
                        gla_chunk)
        y = y[:, :T].reshape(n, SSM_INNER)
        o = o[:, :T].reshape(n, GLA_HEADS * GLA_DV)
        x = _proj_mlp(x, y, o, w_oab, row(g_mlp[0]), w_up[0], w_dn[0], row(g_final), tm_big, False)

        tg = _ceil_to(T, GATE_CHUNK)
        head_shape = (1, bsz, T, FOX_HEADS, FOX_HEAD_DIM)
        if sample:
            q, k, v, u, f = _norm_proj(x, row(g_mix[1]), w_cd, CD_SECTIONS, tm)
            lf_t, ct = _fox_gate(_pad_t(f, bsz, T, tg), bfp)
            q3, k3, v3 = (a.reshape(bsz, T, width) for a in (q, k, v))
            att = _fox_sample(page_table, q3, k3, v3, ct, cache_k, cache_v, lft)
            k_out, v_out = k.reshape(head_shape), v.reshape(head_shape)
        else:
            q, k, kt, vt, u, f = _norm_proj(x, row(g_mix[1]), w_cd, CD_SECTIONS, tm, ("n", "nt", "t", "n", "n"), T)
            lf_t, ct = _fox_gate(_pad_t(f, bsz, T, tg), bfp)
            att = _fox_prompt(q.reshape(bsz, T, width), k.reshape(bsz, T, width), vt, ct)
            to_rows = lambda a: jnp.transpose(a.reshape(1, bsz, FOX_HEADS, FOX_HEAD_DIM, T), (0, 1, 4, 2, 3))
            k_out, v_out = to_rows(kt), to_rows(vt)
        lf = jnp.swapaxes(lf_t, 1, 2)[:, :T]
        cmod, conf_new = _conf(u.reshape(bsz, T, 2 * CONF_CH), conf0, conf_conv_w[0], row(conf_conv_b[0]),
                               row(conf_ln_g[0]), row(conf_ln_b[0]), min(T, CONF_CHUNK))
        x = _proj_mlp(x, att.reshape(n, width), cmod.reshape(n, CONF_CH), w_ocd, row(g_mlp[1]), w_up[1], w_dn[1],
                      row(g_final), tm_big, True)
        return (x.reshape(bsz, T, D_MODEL), h_new.reshape(1, bsz, SSM_HEADS, SSM_HEAD_DIM, SSM_STATE), conv_new[None],
                s_new.reshape(1, bsz, GLA_HEADS, GLA_DK, GLA_DV), k_out, v_out, lf[None], conf_new[None])

    yp, *rest_p = trunk(x_prompt, False)
    ys, *rest_s = trunk(x_sample, True)
    return (yp, ys, *rest_p, *rest_s)
```

```python
import functools

import jax
import jax.numpy as jnp
from jax import lax
from jax.experimental import pallas as pl
from jax.experimental.pallas import tpu as pltpu

F32 = jnp.float32
BF16 = jnp.bfloat16
EPS = 1e-6
LOG2E = 1.4426950408889634

LANES = 128
SUBLANES = 8
MIB = 1024 * 1024

D_MODEL = 1024
D_FF = 4 * D_MODEL
SSM_HEADS = 16
SSM_HEAD_DIM = 64
SSM_INNER = SSM_HEADS * SSM_HEAD_DIM
SSM_GROUPS = 2
SSM_STATE = 128
SSM_CONV = 4
SSM_CONV_CH = SSM_INNER + 2 * SSM_GROUPS * SSM_STATE
GLA_HEADS = 4
GLA_DK = 128
GLA_DV = 256
GLA_RANK = 16
GLA_GATE_NORM = 16.0
FOX_HEADS = 16
FOX_HEAD_DIM = 64
CONF_CH = 512
CONF_WIDTH = 31
PAGE_SIZE = 128

AB_SPLIT = (SSM_INNER, SSM_CONV_CH, SSM_HEADS, GLA_HEADS * GLA_DK, GLA_HEADS * GLA_DK,
            GLA_HEADS * GLA_DV, GLA_HEADS * GLA_DV, GLA_RANK)
CD_SPLIT = (FOX_HEADS * FOX_HEAD_DIM,) * 3 + (FOX_HEADS, 2 * CONF_CH)
AB_SECTIONS = (SSM_INNER, SSM_CONV_CH, GLA_HEADS * GLA_DK, GLA_HEADS * GLA_DK, GLA_HEADS * GLA_DV,
               GLA_HEADS * GLA_DV, LANES, LANES)
CD_SECTIONS = (1024, 1024, 1024, 2 * CONF_CH, LANES)

SSD_CHUNK = 128
GLA_CHUNK = 64
GLA_STEP = 128
SHORT_CHUNK = 16
GATE_CHUNK = 128
GATE_ROWS = 512
FOX_TQ = 256
FOX_TK = 256
FOX_SLOTS = 4
CONF_CHUNK = 256
CONF_PAD = 32
CONF_ROWS = 32
SSD_PAD = 8
FF_CHUNK = 1024
PAGES_PER_STEP = 16


def _dot(a, b):
    return jnp.dot(a, b, preferred_element_type=F32)


def _dot_nt(a, b):
    return lax.dot_general(a, b, (((1,), (1,)), ((), ())), preferred_element_type=F32)


def _dot_tn(a, b):
    return lax.dot_general(a, b, (((0,), (0,)), ((), ())), preferred_element_type=F32)


def _split3(x):
    hi = x.astype(BF16)
    r1 = x - hi.astype(F32)
    mid = r1.astype(BF16)
    lo = (r1 - mid.astype(F32)).astype(BF16)
    return hi, mid, lo


def _dot_exact_l(m, x):
    hi, mid, lo = _split3(x)
    return _dot(m, lo) + _dot(m, mid) + _dot(m, hi)


def _dot_exact_r(x, m):
    hi, mid, lo = _split3(x)
    return _dot(lo, m) + _dot(mid, m) + _dot(hi, m)


def _sigmoid(x):
    return 1.0 / (1.0 + jnp.exp(-x))


def _silu(x):
    return x * _sigmoid(x)


def _softplus(x):
    return jnp.maximum(x, 0.0) + jnp.log1p(jnp.exp(-jnp.abs(x)))


def _log_sigmoid(x):
    return jnp.minimum(x, 0.0) - jnp.log1p(jnp.exp(-jnp.abs(x)))


def _rms(x, g):
    return x * lax.rsqrt(jnp.mean(x * x, axis=-1, keepdims=True) + EPS) * g


def _iota(shape, dim):
    return lax.broadcasted_iota(jnp.int32, shape, dim)


def _tril(n):
    return _iota((n, n), 1) <= _iota((n, n), 0)


def _params(semantics, vmem_mib):
    return pltpu.CompilerParams(dimension_semantics=semantics, vmem_limit_bytes=vmem_mib * MIB)


def _resident(shape):
    nd = len(shape)
    return pl.BlockSpec(shape, lambda *_: (0,) * nd, pipeline_mode=pl.Buffered(1))


def _norm_proj_kernel(x_ref, g_ref, w_ref, *out_refs, sections):
    xb = _rms(x_ref[...], g_ref[...]).astype(BF16)
    outs = iter(out_refs)
    for off, width, mode in sections:
        y = _dot(xb, w_ref[:, off:off + width])
        if "n" in mode:
            next(outs)[...] = y
        if "t" in mode:
            next(outs)[...] = y.T


def _norm_proj(x, g, w, widths, tm, modes=None, seq=None):
    n = x.shape[0]
    modes = modes or ("n",) * len(widths)
    offs = [sum(widths[:i]) for i in range(len(widths))]
    kern = functools.partial(_norm_proj_kernel, sections=tuple(zip(offs, widths, modes)))
    wtot = w.shape[1]
    out_specs, out_shape = [], []
    for s, mode in zip(widths, modes):
        if "n" in mode:
            out_specs.append(pl.BlockSpec((tm, s), lambda i: (i, 0)))
            out_shape.append(jax.ShapeDtypeStruct((n, s), F32))
        if "t" in mode:
            per_seq = seq // tm
            out_specs.append(pl.BlockSpec((None, s, tm), lambda i: (i // per_seq, 0, i % per_seq)))
            out_shape.append(jax.ShapeDtypeStruct((n // seq, s, seq), F32))
    n_out = sum(s * len(mode) for s, mode in zip(widths, modes))
    vmem = (2 * tm * D_MODEL * 4 + D_MODEL * wtot * 2 + 3 * tm * n_out * 4) // MIB + 4
    return pl.pallas_call(
        kern, grid=(n // tm,),
        in_specs=[pl.BlockSpec((tm, D_MODEL), lambda i: (i, 0)), _resident((1, D_MODEL)), _resident(w.shape)],
        out_specs=out_specs, out_shape=out_shape,
        compiler_params=_params(("arbitrary",), vmem), name="norm_proj")(x, g, w)


def _ssd_kernel(z_ref, xbc_ref, dt_ref, conv0_ref, h0_ref, cw_ref, cb_ref, dtb_ref, alog_ref, dsk_ref, ng_ref,
                e16_ref, y_ref, hout_ref, convout_ref, xbuf, h_scr, y_scr, *, chunk, n_chunks, t_valid):
    c = pl.program_id(1)
    L = chunk
    P0 = SSD_PAD - (SSM_CONV - 1)
    last_valid = t_valid - (n_chunks - 1) * L

    @pl.when(c == 0)
    def _():
        xbuf[P0:SSD_PAD, :] = conv0_ref[...]
        h_scr[...] = h0_ref[...]

    xbuf[SSD_PAD:SSD_PAD + L, :] = xbc_ref[...]
    cw = cw_ref[...]
    conv = cb_ref[...] + xbuf[P0:P0 + L, :] * cw[0:1, :]
    for j in range(1, SSM_CONV):
        conv = conv + xbuf[P0 + j:P0 + j + L, :] * cw[j:j + 1, :]
    act = _silu(conv)
    xs = act[:, :SSM_INNER]
    bm_b = act[:, SSM_INNER:SSM_INNER + SSM_GROUPS * SSM_STATE].astype(BF16)
    cm_b = act[:, SSM_INNER + SSM_GROUPS * SSM_STATE:].astype(BF16)

    row = _iota((L, LANES), 0)
    lane = _iota((L, LANES), 1)
    live = (lane < SSM_HEADS) & (c * L + row < t_valid)
    dt = jnp.where(live, _softplus(dt_ref[...] + dtb_ref[...]), 0.0)
    a = -jnp.exp(alog_ref[...]) * dt
    causal = _tril(L)
    tri = jnp.where(causal, 1.0, 0.0).astype(BF16)
    ac = _dot_exact_l(tri, a)
    e16 = e16_ref[...]
    ac_x = _dot_exact_r(ac, e16)
    dt_x = _dot_exact_r(dt, e16)
    if L == LANES:
        ac_t = ac.T
    else:
        ac_t = jnp.concatenate([ac, jnp.zeros((LANES - L, LANES), F32)], axis=0).T[:, :L]
    ac_last = ac[L - 1:L, :]
    ac_last_x = ac_x[L - 1:L, :]
    to_end_x = jnp.exp(ac_last_x - ac_x)
    eac_x = jnp.exp(ac_x)
    xdt = xs * dt_x
    xdt_b = xdt.astype(BF16)
    xend_b = (xdt * to_end_x).astype(BF16)
    lane_lo = lane < SSM_HEAD_DIM
    row_lo = _iota((LANES, LANES), 0) < SSM_HEAD_DIM
    heads_per_group = SSM_HEADS // SSM_GROUPS
    for g in range(SSM_GROUPS):
        bg = bm_b[:, g * SSM_STATE:(g + 1) * SSM_STATE]
        cg = cm_b[:, g * SSM_STATE:(g + 1) * SSM_STATE]
        cb = _dot_nt(cg, bg)
        for p in range(g * heads_per_group // 2, (g + 1) * heads_per_group // 2):
            sl = slice(p * LANES, (p + 1) * LANES)
            ys = []
            for j in range(2):
                h = 2 * p + j
                seg = ac[:, h:h + 1] - ac_t[h:h + 1, :]
                m = (cb * jnp.exp(jnp.where(causal, seg, -jnp.inf))).astype(BF16)
                ys.append(_dot(m, xdt_b[:, sl]))
            y_diag = jnp.where(lane_lo, ys[0], ys[1])
            s_old = h_scr[sl, :]
            y_off = _dot_nt(cg, s_old.astype(BF16)) * eac_x[:, sl]
            dec = jnp.exp(jnp.where(row_lo, ac_last[:, 2 * p:2 * p + 1], ac_last[:, 2 * p + 1:2 * p + 2]))
            h_scr[sl, :] = s_old * dec + _dot_tn(xend_b[:, sl], bg)
            y_scr[:, sl] = y_diag + y_off + dsk_ref[:, sl] * xs[:, sl]

    y_ref[...] = _rms(y_scr[...] * _silu(z_ref[...]), ng_ref[...])

    @pl.when(c == n_chunks - 1)
    def _():
        hout_ref[...] = h_scr[...]
        convout_ref[...] = xbuf[P0 + last_valid:SSD_PAD + last_valid, :]

    tail = xbuf[P0 + L:SSD_PAD + L, :]
    xbuf[P0:SSD_PAD, :] = tail


def _ssd(z, xbc, dtp, conv0, h0, cw, cb, dtb, alog, dsk, ng, e16, t_valid, chunk):
    bsz, t_pad, _ = z.shape
    L = chunk
    nc = t_pad // L
    kern = functools.partial(_ssd_kernel, chunk=chunk, n_chunks=nc, t_valid=t_valid)
    tok = lambda w: pl.BlockSpec((None, L, w), lambda b, c: (b, c, 0))
    per_b = lambda s: pl.BlockSpec((None,) + s, lambda b, c: (b, 0, 0))
    return pl.pallas_call(
        kern, grid=(bsz, nc),
        in_specs=[tok(SSM_INNER), tok(SSM_CONV_CH), tok(LANES), per_b((SSM_CONV - 1, SSM_CONV_CH)),
                  per_b((SSM_INNER, SSM_STATE)), _resident(cw.shape), _resident(cb.shape), _resident(dtb.shape),
                  _resident(alog.shape), _resident(dsk.shape), _resident(ng.shape), _resident(e16.shape)],
        out_specs=[tok(SSM_INNER), per_b((SSM_INNER, SSM_STATE)), per_b((SSM_CONV - 1, SSM_CONV_CH))],
        out_shape=[jax.ShapeDtypeStruct((bsz, t_pad, SSM_INNER), F32),
                   jax.ShapeDtypeStruct((bsz, SSM_INNER, SSM_STATE), F32),
                   jax.ShapeDtypeStruct((bsz, SSM_CONV - 1, SSM_CONV_CH), F32)],
        scratch_shapes=[pltpu.VMEM((SSD_PAD + L, SSM_CONV_CH), F32), pltpu.VMEM((SSM_INNER, SSM_STATE), F32),
                        pltpu.VMEM((L, SSM_INNER), F32)],
        compiler_params=_params(("arbitrary", "arbitrary"), 40), name="ssd")(
            z, xbc, dtp, conv0, h0, cw, cb, dtb, alog, dsk, ng, e16)


def _gla_kernel(q_ref, k_ref, v_ref, g_ref, glr_ref, s0_ref, w2_ref, gb_ref, ng_ref, o_ref, sout_ref, s_scr,
                *, step_rows, chunk, n_chunks, t_valid):
    c = pl.program_id(1)
    R, L = step_rows, chunk
    n_sub = R // L
    chunk_shift = L.bit_length() - 1
    width = GLA_HEADS * GLA_DK

    @pl.when(c == 0)
    def _():
        s_scr[...] = s0_ref[...]

    x = _dot(glr_ref[...].astype(BF16), w2_ref[...]) + gb_ref[...]
    logf = _log_sigmoid(x) * (1.0 / GLA_GATE_NORM)
    logf = jnp.where(c * R + _iota((R, width), 0) < t_valid, logf, 0.0)
    row, col = _iota((R, R), 0), _iota((R, R), 1)
    causal = ((row >> chunk_shift) == (col >> chunk_shift)) & (col <= row)
    tri = jnp.where(causal, 1.0, 0.0).astype(BF16)
    bcum = _dot_exact_l(tri, logf)
    bl = [bcum[(s + 1) * L - 1:(s + 1) * L, :] for s in range(n_sub)]
    bl_rows = jnp.concatenate([jnp.broadcast_to(b, (L, width)) for b in bl], axis=0)
    q_dec = q_ref[...] * (GLA_DK ** -0.5) * jnp.exp(bcum)
    k = k_ref[...]
    k_inv = k * jnp.exp(-bcum)
    k_end = k * jnp.exp(bl_rows - bcum)
    for h in range(GLA_HEADS):
        ks = slice(h * GLA_DK, (h + 1) * GLA_DK)
        vs = slice(h * GLA_DV, (h + 1) * GLA_DV)
        qd = q_dec[:, ks].astype(BF16)
        ke = k_end[:, ks].astype(BF16)
        vb = v_ref[:, vs].astype(BF16)
        att = jnp.where(causal, _dot_nt(qd, k_inv[:, ks].astype(BF16)), 0.0)
        o = _dot(att.astype(BF16), vb)
        state = s_scr[ks, :]
        inter = []
        for s in range(n_sub):
            rs = slice(s * L, (s + 1) * L)
            inter.append(_dot(qd[rs], state.astype(BF16)))
            dcol = jnp.broadcast_to(jnp.exp(bl[s][:, ks]), (GLA_DK, GLA_DK)).T
            state = state * jnp.concatenate([dcol, dcol], axis=1) + _dot_tn(ke[rs], vb[rs])
        s_scr[ks, :] = state
        o = o + jnp.concatenate(inter, axis=0)
        o_ref[:, vs] = _rms(o, ng_ref[...]) * _silu(g_ref[:, vs])

    @pl.when(c == n_chunks - 1)
    def _():
        sout_ref[...] = s_scr[...]


def _gla(q, k, v, g, glr, s0, w2, gb, ng, t_valid, step_rows, chunk):
    bsz, t_pad, _ = q.shape
    L = step_rows
    nc = t_pad // L
    kern = functools.partial(_gla_kernel, step_rows=step_rows, chunk=chunk, n_chunks=nc, t_valid=t_valid)
    tok = lambda w: pl.BlockSpec((None, L, w), lambda b, c: (b, c, 0))
    st = pl.BlockSpec((None, GLA_HEADS * GLA_DK, GLA_DV), lambda b, c: (b, 0, 0))
    return pl.pallas_call(
        kern, grid=(bsz, nc),
        in_specs=[tok(GLA_HEADS * GLA_DK), tok(GLA_HEADS * GLA_DK), tok(GLA_HEADS * GLA_DV), tok(GLA_HEADS * GLA_DV),
                  tok(LANES), st, _resident(w2.shape), _resident(gb.shape), _resident(ng.shape)],
        out_specs=[tok(GLA_HEADS * GLA_DV), st],
        out_shape=[jax.ShapeDtypeStruct((bsz, t_pad, GLA_HEADS * GLA_DV), F32),
                   jax.ShapeDtypeStruct((bsz, GLA_HEADS * GLA_DK, GLA_DV), F32)],
        scratch_shapes=[pltpu.VMEM((GLA_HEADS * GLA_DK, GLA_DV), F32)],
        compiler_params=_params(("arbitrary", "arbitrary"), 32), name="gla")(q, k, v, g, glr, s0, w2, gb, ng)


def _proj_mlp_kernel(x_ref, a_ref, b_ref, wo_ref, g_ref, wu_ref, wd_ref, gf_ref, o_ref, *, wa, final_norm):
    x = (x_ref[...] + _dot(a_ref[...].astype(BF16), wo_ref[:wa, :])
         + _dot(b_ref[...].astype(BF16), wo_ref[wa:, :]))
    o_ref[...] = x
    xb = _rms(o_ref[...], g_ref[...]).astype(BF16)
    for c in range(D_FF // FF_CHUNK):
        sl = slice(c * FF_CHUNK, (c + 1) * FF_CHUNK)
        h = jnp.maximum(_dot(xb, wu_ref[:, sl]), 0.0)
        o_ref[...] += _dot((h * h).astype(BF16), wd_ref[sl, :])
    if final_norm:
        o_ref[...] = _rms(o_ref[...], gf_ref[...])


def _proj_mlp(x, a, b, wo, g, wu, wd, gf, tm, final_norm):
    n = x.shape[0]
    wa, wb = a.shape[1], b.shape[1]
    row = lambda s: pl.BlockSpec((tm, s), lambda i: (i, 0))
    weights = (wo.size + wu.size + wd.size) * 2
    tiles = 2 * tm * (2 * D_MODEL + wa + wb) * 4 + tm * (2 * D_MODEL + 2 * FF_CHUNK) * 4
    return pl.pallas_call(
        functools.partial(_proj_mlp_kernel, wa=wa, final_norm=final_norm), grid=(n // tm,),
        in_specs=[row(D_MODEL), row(wa), row(wb), _resident(wo.shape), _resident(g.shape), _resident(wu.shape),
                  _resident(wd.shape), _resident(gf.shape)],
        out_specs=row(D_MODEL), out_shape=jax.ShapeDtypeStruct((n, D_MODEL), F32),
        compiler_params=_params(("arbitrary",), (weights + tiles) // MIB + 6), name="proj_mlp")(
            x, a, b, wo, g, wu, wd, gf)


def _fox_gate_kernel(f_ref, bf_ref, lft_ref, ct_ref, carry, *, rows):
    @pl.when(pl.program_id(1) == 0)
    def _():
        carry[...] = jnp.zeros_like(carry)

    L = GATE_CHUNK
    tri = jnp.where(_tril(L), 1.0, 0.0).astype(BF16)
    run = carry[...]
    for r0 in range(0, rows, L):
        lf = _log_sigmoid(f_ref[r0:r0 + L, :] + bf_ref[...])
        cblk = run + _dot_exact_l(tri, lf)
        run = cblk[L - 1:L, :]
        lft_ref[:, r0:r0 + L] = lf.T[:FOX_HEADS, :]
        ct_ref[:, r0:r0 + L] = cblk.T[:FOX_HEADS, :]
    carry[...] = run


def _fox_gate(f, bf, rows):
    bsz, t_pad, _ = f.shape
    L = rows
    return pl.pallas_call(
        functools.partial(_fox_gate_kernel, rows=rows), grid=(bsz, t_pad // L),
        in_specs=[pl.BlockSpec((None, L, LANES), lambda b, t: (b, t, 0)), _resident(bf.shape)],
        out_specs=[pl.BlockSpec((None, FOX_HEADS, L), lambda b, t: (b, 0, t)),
                   pl.BlockSpec((None, FOX_HEADS, L), lambda b, t: (b, 0, t))],
        out_shape=[jax.ShapeDtypeStruct((bsz, FOX_HEADS, t_pad), F32),
                   jax.ShapeDtypeStruct((bsz, FOX_HEADS, t_pad), F32)],
        scratch_shapes=[pltpu.VMEM((1, LANES), F32)],
        compiler_params=_params(("arbitrary", "arbitrary"), 16), name="fox_gate")(f, bf)


def _fox_prompt_kernel(q_ref, k_ref, vt_ref, ct_ref, o_ref, kb_scr, vtb_scr, ck_scr, *score_scr, n_q_blocks):
    hp = pl.program_id(1)
    tq, tk = FOX_TQ, FOX_TK
    dh = FOX_HEAD_DIM
    heads = [2 * hp, 2 * hp + 1]

    kb_scr[...] = k_ref[...].astype(BF16)
    vtb_scr[...] = vt_ref[...].astype(BF16)
    for j in range(2):
        c_row = ct_ref[pl.ds(heads[j], 1), :] * LOG2E
        ck_scr[j] = jnp.broadcast_to(c_row, (LANES, c_row.shape[1])).T

    lane = _iota((tq, LANES), 1)
    kpq = tq // tk
    causal = [_iota((tk, tq), 0) + d * tk <= _iota((tk, tq), 1) for d in range(kpq)]

    def fold8(x, op):
        parts = [x[r * 8:(r + 1) * 8, :] for r in range(x.shape[0] // 8)]
        while len(parts) > 1:
            parts = [op(parts[i], parts[i + 1]) for i in range(0, len(parts), 2)]
        return parts[0]

    st = [dict() for _ in range(n_q_blocks)]
    bufs = lambda v, j: (score_scr[2 * (v % 2) + j], score_scr[FOX_SLOTS + 2 * (v % 2) + j])

    def p1_block(v, ki):
        if ki == 0:
            q = q_ref[v * tq:(v + 1) * tq, :] * (dh ** -0.5 * LOG2E)
            st[v]["qm"] = [jnp.where(lane < dh, q, 0.0).astype(BF16), jnp.where(lane >= dh, q, 0.0).astype(BF16)]
            st[v]["mx"] = [jnp.full((8, tq), -jnp.inf, F32)] * 2
        rows = slice(ki * tk, (ki + 1) * tk)
        for j in range(2):
            s = _dot_nt(kb_scr[rows, :], st[v]["qm"][j]) - jnp.concatenate([ck_scr[j, rows, :]] * (tq // LANES), axis=1)
            if ki >= v * kpq:
                s = jnp.where(causal[ki - v * kpq], s, -jnp.inf)
            bufs(v, j)[0][rows, :] = s
            st[v]["mx"][j] = jnp.maximum(st[v]["mx"][j], fold8(s, jnp.maximum))

    def p1_finish(v):
        cq = [ct_ref[pl.ds(heads[j], 1), v * tq:(v + 1) * tq] * LOG2E for j in range(2)]
        m_new = [jnp.max(st[v]["mx"][j], axis=0, keepdims=True) + cq[j] for j in range(2)]
        st[v]["shift"] = [m_new[j] - cq[j] for j in range(2)]
        st[v]["lsum"] = [jnp.zeros((8, tq), F32)] * 2

    def p2_block(v, ki):
        rows = slice(ki * tk, (ki + 1) * tk)
        for j in range(2):
            s_scr, p_scr = bufs(v, j)
            p = jnp.exp2(s_scr[rows, :] - st[v]["shift"][j])
            st[v]["lsum"][j] = st[v]["lsum"][j] + fold8(p, jnp.add)
            p_scr[rows, :] = p.astype(BF16)

    def p2_finish(v):
        n_k = (v + 1) * tq
        outs = []
        for j in range(2):
            acc = _dot(vtb_scr[j * dh:(j + 1) * dh, 0:n_k], bufs(v, j)[1][0:n_k, :])
            outs.append(acc / jnp.sum(st[v]["lsum"][j], axis=0, keepdims=True))
        o_ref[v * tq:(v + 1) * tq, :] = jnp.concatenate(outs, axis=0).T

    def stage(block, finish, v):
        return [functools.partial(block, v, ki) for ki in range((v + 1) * kpq)] + [functools.partial(finish, v)]

    for step in stage(p1_block, p1_finish, 0):
        step()
    for v in range(n_q_blocks):
        ahead = stage(p1_block, p1_finish, v + 1) if v + 1 < n_q_blocks else []
        behind = stage(p2_block, p2_finish, v)
        for i in range(max(len(ahead), len(behind))):
            for steps in (ahead, behind):
                if i < len(steps):
                    steps[i]()


def _fox_prompt(q, k, vt, ct):
    bsz, T, _ = q.shape
    tq = FOX_TQ
    return pl.pallas_call(
        functools.partial(_fox_prompt_kernel, n_q_blocks=T // tq), grid=(bsz, FOX_HEADS // 2),
        in_specs=[pl.BlockSpec((None, T, LANES), lambda b, hp: (b, 0, hp)),
                  pl.BlockSpec((None, T, LANES), lambda b, hp: (b, 0, hp)),
                  pl.BlockSpec((None, LANES, T), lambda b, hp: (b, hp, 0)),
                  pl.BlockSpec((None, FOX_HEADS, T), lambda b, hp: (b, 0, 0))],
        out_specs=pl.BlockSpec((None, T, LANES), lambda b, hp: (b, 0, hp)),
        out_shape=jax.ShapeDtypeStruct((bsz, T, FOX_HEADS * FOX_HEAD_DIM), F32),
        scratch_shapes=([pltpu.VMEM((T, LANES), BF16), pltpu.VMEM((LANES, T), BF16), pltpu.VMEM((2, T, LANES), F32)]
                        + [pltpu.VMEM((T, tq), F32)] * FOX_SLOTS + [pltpu.VMEM((T, tq), BF16)] * FOX_SLOTS),
        compiler_params=_params(("arbitrary",) * 2, 48), name="fox_prompt")(q, k, vt, ct)


def _fox_sample_kernel(pt_ref, q_ref, kn_ref, vn_ref, cnt_ref, *refs, n_steps):
    G = PAGES_PER_STEP
    k_refs, v_refs, lf_refs = refs[:G], refs[G:2 * G], refs[2 * G:3 * G]
    o_ref, qbd, kt_scr, vt_scr, m_scr, l_scr, acc_scr, carry, cncol = refs[3 * G:]
    p = pl.program_id(1)
    R = LANES
    T = R // FOX_HEADS
    width = FOX_HEADS * FOX_HEAD_DIM
    dh_shift = FOX_HEAD_DIM.bit_length() - 1
    t_shift = T.bit_length() - 1
    row = _iota((R, LANES), 0)
    lane = _iota((R, LANES), 1)

    def expand_heads(xt):
        return jnp.concatenate([jnp.broadcast_to(xt[h:h + 1, :], (T, xt.shape[1])) for h in range(FOX_HEADS)], axis=0)

    def update(s, vb):
        m = m_scr[...]
        cq = cncol[...]
        m_new = jnp.maximum(m, jnp.max(s, axis=-1, keepdims=True) + cq)
        pr = jnp.exp(s - (m_new - cq))
        alpha = jnp.exp(m - m_new)
        m_scr[...] = m_new
        l_scr[...] = alpha * l_scr[...] + jnp.sum(pr, axis=-1, keepdims=True)
        acc_scr[...] = alpha * acc_scr[...] + _dot(pr.astype(BF16), vb)

    @pl.when(p == 0)
    def _():
        q = q_ref[...] * (FOX_HEAD_DIM ** -0.5)
        qt = jnp.concatenate([q] * FOX_HEADS, axis=0)
        own = (_iota((R, width), 1) >> dh_shift) == (_iota((R, width), 0) >> t_shift)
        qbd[...] = jnp.where(own, qt, 0.0).astype(BF16)
        m_scr[...] = jnp.full_like(m_scr, -jnp.inf)
        l_scr[...] = jnp.zeros_like(l_scr)
        acc_scr[...] = jnp.zeros_like(acc_scr)
        carry[...] = jnp.zeros_like(carry)
        cn = expand_heads(cnt_ref[...])
        t_of_row = row & (T - 1)
        cncol[...] = jnp.sum(jnp.where(lane == t_of_row, cn, 0.0), axis=-1, keepdims=True)
        pad = jnp.zeros((LANES - T, width), F32)
        kb = jnp.concatenate([kn_ref[...], pad], axis=0).astype(BF16)
        vb = jnp.concatenate([vn_ref[...], pad], axis=0).astype(BF16)
        s = _dot_nt(qbd[...], kb) - cn
        update(jnp.where(lane <= t_of_row, s, -jnp.inf), vb)

    for i in range(G):
        kt_scr[:, i * PAGE_SIZE:(i + 1) * PAGE_SIZE] = k_refs[i][...].astype(BF16)
        vt_scr[:, i * PAGE_SIZE:(i + 1) * PAGE_SIZE] = v_refs[i][...].astype(BF16)
    later = jnp.where(_iota((LANES, LANES), 0) > _iota((LANES, LANES), 1), 1.0, 0.0).astype(BF16)
    lf_all = jnp.concatenate([lf_refs[i][...] for i in range(G)], axis=0)
    within = _dot_exact_r(lf_all, later)
    total = jnp.sum(lf_all, axis=-1, keepdims=True)
    run = carry[...]
    sufs = []
    for i in range(G):
        sufs.append(run + within[i * FOX_HEADS:(i + 1) * FOX_HEADS, :])
        run = run + total[i * FOX_HEADS:(i + 1) * FOX_HEADS, :]
    carry[...] = run
    s = _dot(qbd[...], kt_scr[...]) + expand_heads(jnp.concatenate(sufs, axis=1))
    m = m_scr[...]
    cq = cncol[...]
    m_new = jnp.maximum(m, jnp.max(s, axis=-1, keepdims=True) + cq)
    pr = jnp.exp(s - (m_new - cq))
    alpha = jnp.exp(m - m_new)
    m_scr[...] = m_new
    l_scr[...] = alpha * l_scr[...] + jnp.sum(pr, axis=-1, keepdims=True)
    acc_scr[...] = alpha * acc_scr[...] + _dot_nt(pr.astype(BF16), vt_scr[...])

    @pl.when(p == n_steps - 1)
    def _():
        an = acc_scr[...] / l_scr[...]
        out = jnp.zeros((T, width), F32)
        col_head = _iota((T, width), 1) >> dh_shift
        for h in range(FOX_HEADS):
            out = jnp.where(col_head == h, an[h * T:(h + 1) * T, :], out)
        o_ref[...] = out


def _fox_sample(page_table, q, kn, vn, cnt, cache_k, cache_v, lft):
    bsz, T, width = q.shape
    n_pages = page_table.shape[1]
    G = PAGES_PER_STEP
    n_steps = n_pages // G

    def page(i):
        return lambda b, p, pt: (pt[b, n_pages - 1 - (p * G + i)], 0, 0)

    per_b = lambda s: pl.BlockSpec((None,) + s, lambda b, p, pt: (b, 0, 0))
    in_specs = ([per_b((T, width)), per_b((T, width)), per_b((T, width)), per_b((FOX_HEADS, LANES))]
                + [pl.BlockSpec((None, width, PAGE_SIZE), page(i)) for i in range(G)]
                + [pl.BlockSpec((None, width, PAGE_SIZE), page(i)) for i in range(G)]
                + [pl.BlockSpec((None, FOX_HEADS, PAGE_SIZE), page(i)) for i in range(G)])
    grid_spec = pltpu.PrefetchScalarGridSpec(
        num_scalar_prefetch=1, grid=(bsz, n_steps), in_specs=in_specs, out_specs=per_b((T, width)),
        scratch_shapes=[pltpu.VMEM((LANES, width), BF16), pltpu.VMEM((width, G * PAGE_SIZE), BF16),
                        pltpu.VMEM((width, G * PAGE_SIZE), BF16), pltpu.VMEM((LANES, 1), F32),
                        pltpu.VMEM((LANES, 1), F32), pltpu.VMEM((LANES, width), F32), pltpu.VMEM((FOX_HEADS, 1), F32),
                        pltpu.VMEM((LANES, 1), F32)])
    return pl.pallas_call(
        functools.partial(_fox_sample_kernel, n_steps=n_steps), grid_spec=grid_spec,
        out_shape=jax.ShapeDtypeStruct((bsz, T, width), F32),
        compiler_params=_params(("arbitrary", "arbitrary"), 56), name="fox_sample")(
            page_table, q, kn, vn, cnt, *([cache_k] * G), *([cache_v] * G), *([lft] * G))


def _conf_kernel(u_ref, st0_ref, cw_ref, cb_ref, lg_ref, lb_ref, c_ref, stout_ref, buf, sh, *, chunk, n_chunks):
    t = pl.program_id(1)
    L = chunk
    P0 = CONF_PAD - (CONF_WIDTH - 1)

    @pl.when(t == 0)
    def _():
        buf[P0:CONF_PAD, :] = st0_ref[...]

    u = u_ref[...]
    buf[CONF_PAD:CONF_PAD + L, :] = u[:, :CONF_CH] * _sigmoid(u[:, CONF_CH:])
    span = L + CONF_PAD - SUBLANES
    for b in range(1, SUBLANES):
        sh[b - 1, 0:span, :] = buf[b:b + span, :]
    cw = cw_ref[...]
    rb = min(L, CONF_ROWS)
    for r0 in range(0, L, rb):
        acc = jnp.broadcast_to(cb_ref[...], (rb, CONF_CH))
        for j in range(CONF_WIDTH):
            a, b = divmod(P0 + j, SUBLANES)
            lo = r0 + a * SUBLANES
            src = buf[lo:lo + rb, :] if b == 0 else sh[b - 1, lo:lo + rb, :]
            acc = acc + src * cw[j:j + 1, :]
        xc = acc - jnp.mean(acc, axis=-1, keepdims=True)
        var = jnp.mean(xc * xc, axis=-1, keepdims=True)
        c_ref[r0:r0 + rb, :] = _silu(xc * lax.rsqrt(var + EPS) * lg_ref[...] + lb_ref[...])

    @pl.when(t == n_chunks - 1)
    def _():
        stout_ref[...] = buf[P0 + L:CONF_PAD + L, :]

    tail = buf[L:L + CONF_PAD, :]
    buf[0:CONF_PAD, :] = tail


def _conf(u, st0, cw, cb, lg, lb, chunk):
    bsz, T, _ = u.shape
    nc = T // chunk
    st = pl.BlockSpec((None, CONF_WIDTH - 1, CONF_CH), lambda b, t: (b, 0, 0))
    return pl.pallas_call(
        functools.partial(_conf_kernel, chunk=chunk, n_chunks=nc), grid=(bsz, nc),
        in_specs=[pl.BlockSpec((None, chunk, 2 * CONF_CH), lambda b, t: (b, t, 0)), st, _resident(cw.shape),
                  _resident(cb.shape), _resident(lg.shape), _resident(lb.shape)],
        out_specs=[pl.BlockSpec((None, chunk, CONF_CH), lambda b, t: (b, t, 0)), st],
        out_shape=[jax.ShapeDtypeStruct((bsz, T, CONF_CH), F32),
                   jax.ShapeDtypeStruct((bsz, CONF_WIDTH - 1, CONF_CH), F32)],
        scratch_shapes=[pltpu.VMEM((CONF_PAD + chunk, CONF_CH), F32),
                        pltpu.VMEM((SUBLANES - 1, CONF_PAD + chunk - SUBLANES, CONF_CH), F32)],
        compiler_params=_params(("arbitrary", "arbitrary"), 24), name="conf_conv")(u, st0, cw, cb, lg, lb)


def _pad_cols(w, width):
    return jnp.pad(w, ((0, 0), (0, width - w.shape[1])))


def _split_cols(w, sizes):
    out, off = [], 0
    for s in sizes:
        out.append(w[:, off:off + s])
        off += s
    return out


def _pad_t(a, bsz, t, t_pad):
    a = a.reshape(bsz, t, a.shape[-1])
    return a if t_pad == t else jnp.pad(a, ((0, 0), (0, t_pad - t), (0, 0)))


def _ceil_to(x, m):
    return -(-x // m) * m


def kernel(x_prompt, x_sample, cache_fox_k, cache_fox_v, cache_fox_logf, page_table, state_ssm, state_ssm_conv, state_gla, state_conf_conv, g_mix, g_mlp, g_final, w_in_ab, ssm_conv_w, ssm_conv_b, ssm_dt_bias, ssm_a_log, ssm_d, ssm_norm_g, gla_gate_w2, gla_gate_b, gla_norm_g, w_out_ab, w_in_cd, fox_b_f, conf_conv_w, conf_conv_b, conf_ln_g, conf_ln_b, w_out_cd, w_mlp_up, w_mlp_down):
    row = lambda v: v.reshape(1, -1)
    wz, wxbc, wdt, wq, wk, wv, wg, wlr = _split_cols(w_in_ab[0], AB_SPLIT)
    w_ab = jnp.concatenate([wz, wxbc, wq, wk, wv, wg, _pad_cols(wdt, LANES), _pad_cols(wlr, LANES)], axis=1).astype(BF16)
    cq, ck, cv, cf, cu = _split_cols(w_in_cd[0], CD_SPLIT)
    w_cd = jnp.concatenate([cq, ck, cv, cu, _pad_cols(cf, LANES)], axis=1).astype(BF16)
    w_oab = w_out_ab[0].astype(BF16)
    w_ocd = w_out_cd[0].astype(BF16)
    w_up = w_mlp_up.astype(BF16)
    w_dn = w_mlp_down.astype(BF16)
    dtb = _pad_cols(row(ssm_dt_bias[0]), LANES)
    alog = _pad_cols(row(ssm_a_log[0]), LANES)
    dsk = row(jnp.repeat(ssm_d[0], SSM_HEAD_DIM))
    e16 = (jnp.arange(LANES)[:, None] == jnp.arange(SSM_INNER)[None, :] // SSM_HEAD_DIM).astype(BF16)
    w2 = jnp.pad(gla_gate_w2[0], ((0, LANES - GLA_RANK), (0, 0))).astype(BF16)
    bfp = _pad_cols(row(fox_b_f[0]), LANES)
    width = FOX_HEADS * FOX_HEAD_DIM
    n_pool = cache_fox_k.shape[1]
    cache_k = jnp.transpose(cache_fox_k[0], (0, 2, 3, 1)).reshape(n_pool, width, PAGE_SIZE)
    cache_v = jnp.transpose(cache_fox_v[0], (0, 2, 3, 1)).reshape(n_pool, width, PAGE_SIZE)
    lft = jnp.swapaxes(cache_fox_logf[0], 1, 2)

    def trunk(x3, sample):
        bsz, T, _ = x3.shape
        n = bsz * T
        tm = min(n, 512)
        tm_big = min(n, 512)
        x = x3.reshape(n, D_MODEL)
        if sample:
            h0 = state_ssm[0].reshape(bsz, SSM_INNER, SSM_STATE)
            conv0, s0, conf0 = state_ssm_conv[0], state_gla[0].reshape(bsz, GLA_HEADS * GLA_DK, GLA_DV), state_conf_conv[0]
        else:
            h0 = jnp.zeros((bsz, SSM_INNER, SSM_STATE), F32)
            conv0 = jnp.zeros((bsz, SSM_CONV - 1, SSM_CONV_CH), F32)
            s0 = jnp.zeros((bsz, GLA_HEADS * GLA_DK, GLA_DV), F32)
            conf0 = jnp.zeros((bsz, CONF_WIDTH - 1, CONF_CH), F32)

        z, xbc, q, k, v, g, dtp, glr = _norm_proj(x, row(g_mix[0]), w_ab, AB_SECTIONS, tm)
        short = T <= SHORT_CHUNK
        ssd_chunk = SHORT_CHUNK if short else SSD_CHUNK
        gla_rows, gla_chunk = (SHORT_CHUNK, SHORT_CHUNK) if short else (GLA_STEP, GLA_CHUNK)
        ta = _ceil_to(T, ssd_chunk)
        y, h_new, conv_new = _ssd(_pad_t(z, bsz, T, ta), _pad_t(xbc, bsz, T, ta), _pad_t(dtp, bsz, T, ta), conv0, h0,
                                  ssm_conv_w[0], row(ssm_conv_b[0]), dtb, alog, dsk, row(ssm_norm_g[0]), e16, T,
                                  ssd_chunk)
        tb = _ceil_to(T, gla_rows)
        o, s_new = _gla(_pad_t(q, bsz, T, tb), _pad_t(k, bsz, T, tb), _pad_t(v, bsz, T, tb), _pad_t(g, bsz, T, tb),
                        _pad_t(glr, bsz, T, tb), s0, w2, row(gla_gate_b[0]), row(gla_norm_g[0]), T, gla_rows,
                        gla_chunk)
        y = y[:, :T].reshape(n, SSM_INNER)
        o = o[:, :T].reshape(n, GLA_HEADS * GLA_DV)
        x = _proj_mlp(x, y, o, w_oab, row(g_mlp[0]), w_up[0], w_dn[0], row(g_final), tm_big, False)

        tg = _ceil_to(T, GATE_CHUNK)
        gate_rows = min(tg, GATE_ROWS)
        head_shape = (1, bsz, T, FOX_HEADS, FOX_HEAD_DIM)
        if sample:
            q, k, v, u, f = _norm_proj(x, row(g_mix[1]), w_cd, CD_SECTIONS, tm)
            lf_t, ct = _fox_gate(_pad_t(f, bsz, T, tg), bfp, gate_rows)
            q3, k3, v3 = (a.reshape(bsz, T, width) for a in (q, k, v))
            att = _fox_sample(page_table, q3, k3, v3, ct, cache_k, cache_v, lft)
            k_out, v_out = k.reshape(head_shape), v.reshape(head_shape)
        else:
            q, k, kt, vt, u, f = _norm_proj(x, row(g_mix[1]), w_cd, CD_SECTIONS, tm, ("n", "nt", "t", "n", "n"), T)
            lf_t, ct = _fox_gate(_pad_t(f, bsz, T, tg), bfp, gate_rows)
            att = _fox_prompt(q.reshape(bsz, T, width), k.reshape(bsz, T, width), vt, ct)
            to_rows = lambda a: jnp.transpose(a.reshape(1, bsz, FOX_HEADS, FOX_HEAD_DIM, T), (0, 1, 4, 2, 3))
            k_out, v_out = to_rows(kt), to_rows(vt)
        lf = jnp.swapaxes(lf_t, 1, 2)[:, :T]
        cmod, conf_new = _conf(u.reshape(bsz, T, 2 * CONF_CH), conf0, conf_conv_w[0], row(conf_conv_b[0]),
                               row(conf_ln_g[0]), row(conf_ln_b[0]), min(T, CONF_CHUNK))
        x = _proj_mlp(x, att.reshape(n, width), cmod.reshape(n, CONF_CH), w_ocd, row(g_mlp[1]), w_up[1], w_dn[1],
                      row(g_final), tm_big, True)
        return (x.reshape(bsz, T, D_MODEL), h_new.reshape(1, bsz, SSM_HEADS, SSM_HEAD_DIM, SSM_STATE), conv_new[None],
                s_new.reshape(1, bsz, GLA_HEADS, GLA_DK, GLA_DV), k_out, v_out, lf[None], conf_new[None])

    yp, *rest_p = trunk(x_prompt, False)
    ys, *rest_s = trunk(x_sample, True)
    return (yp, ys, *rest_p, *rest_s)
```

```python
import functools

import jax
import jax.numpy as jnp
from jax import lax
from jax.experimental import pallas as pl
from jax.experimental.pallas import tpu as pltpu

F32 = jnp.float32
BF16 = jnp.bfloat16
EPS = 1e-6
LOG2E = 1.4426950408889634

LANES = 128
SUBLANES = 8
MIB = 1024 * 1024

D_MODEL = 1024
D_FF = 4 * D_MODEL
SSM_HEADS = 16
SSM_HEAD_DIM = 64
SSM_INNER = SSM_HEADS * SSM_HEAD_DIM
SSM_GROUPS = 2
SSM_STATE = 128
SSM_CONV = 4
SSM_CONV_CH = SSM_INNER + 2 * SSM_GROUPS * SSM_STATE
GLA_HEADS = 4
GLA_DK = 128
GLA_DV = 256
GLA_RANK = 16
GLA_GATE_NORM = 16.0
FOX_HEADS = 16
FOX_HEAD_DIM = 64
CONF_CH = 512
CONF_WIDTH = 31
PAGE_SIZE = 128

AB_SPLIT = (SSM_INNER, SSM_CONV_CH, SSM_HEADS, GLA_HEADS * GLA_DK, GLA_HEADS * GLA_DK,
            GLA_HEADS * GLA_DV, GLA_HEADS * GLA_DV, GLA_RANK)
CD_SPLIT = (FOX_HEADS * FOX_HEAD_DIM,) * 3 + (FOX_HEADS, 2 * CONF_CH)
AB_SECTIONS = (SSM_INNER, SSM_CONV_CH, GLA_HEADS * GLA_DK, GLA_HEADS * GLA_DK, GLA_HEADS * GLA_DV,
               GLA_HEADS * GLA_DV, LANES, LANES)
CD_SECTIONS = (1024, 1024, 1024, 2 * CONF_CH, LANES)

SSD_CHUNK = 128
GLA_CHUNK = 64
GLA_STEP = 128
SHORT_CHUNK = 16
SHORT_SEQS = 4
GATE_CHUNK = 128
GATE_ROWS = 512
FOX_TQ = 256
FOX_TK = 256
FOX_SLOTS = 4
CONF_CHUNK = 256
CONF_PAD = 32
CONF_ROWS = 32
SSD_PAD = 8
FF_CHUNK = 1024
PAGES_PER_STEP = 16


def _dot(a, b):
    return jnp.dot(a, b, preferred_element_type=F32)


def _dot_nt(a, b):
    return lax.dot_general(a, b, (((1,), (1,)), ((), ())), preferred_element_type=F32)


def _dot_tn(a, b):
    return lax.dot_general(a, b, (((0,), (0,)), ((), ())), preferred_element_type=F32)


def _split3(x):
    hi = x.astype(BF16)
    r1 = x - hi.astype(F32)
    mid = r1.astype(BF16)
    lo = (r1 - mid.astype(F32)).astype(BF16)
    return hi, mid, lo


def _dot_exact_l(m, x):
    hi, mid, lo = _split3(x)
    return _dot(m, lo) + _dot(m, mid) + _dot(m, hi)


def _dot_exact_r(x, m):
    hi, mid, lo = _split3(x)
    return _dot(lo, m) + _dot(mid, m) + _dot(hi, m)


def _sigmoid(x):
    return 1.0 / (1.0 + jnp.exp2(x * (-LOG2E)))


def _silu(x):
    return x * _sigmoid(x)


def _softplus(x):
    return jnp.maximum(x, 0.0) + jnp.log1p(jnp.exp2(jnp.abs(x) * (-LOG2E)))


def _log_sigmoid(x):
    return jnp.minimum(x, 0.0) - jnp.log1p(jnp.exp2(jnp.abs(x) * (-LOG2E)))


def _rms(x, g):
    return x * lax.rsqrt(jnp.mean(x * x, axis=-1, keepdims=True) + EPS) * g


def _iota(shape, dim):
    return lax.broadcasted_iota(jnp.int32, shape, dim)


def _tril(n):
    return _iota((n, n), 1) <= _iota((n, n), 0)


def _params(semantics, vmem_mib):
    return pltpu.CompilerParams(dimension_semantics=semantics, vmem_limit_bytes=vmem_mib * MIB)


def _resident(shape):
    nd = len(shape)
    return pl.BlockSpec(shape, lambda *_: (0,) * nd, pipeline_mode=pl.Buffered(1))


def _norm_proj_kernel(x_ref, g_ref, *refs, n_w, sections):
    w_refs, out_refs = refs[:n_w], refs[n_w:]
    xb = _rms(x_ref[...], g_ref[...]).astype(BF16)
    outs = iter(out_refs)
    for wi, off, width, mode in sections:
        y = _dot(xb, w_refs[wi][:, off:off + width])
        if "n" in mode:
            next(outs)[...] = y
        if "t" in mode:
            next(outs)[...] = y.T


def _norm_proj(x, g, ws, widths, tm, modes=None, seq=None):
    n = x.shape[0]
    modes = modes or ("n",) * len(widths)
    where, wi, off = [], 0, 0
    for s in widths:
        if off == ws[wi].shape[1]:
            wi, off = wi + 1, 0
        where.append((wi, off))
        off += s
    assert wi == len(ws) - 1 and off == ws[wi].shape[1]
    kern = functools.partial(_norm_proj_kernel, n_w=len(ws),
                             sections=tuple((wi, off, s, m) for (wi, off), s, m in zip(where, widths, modes)))
    wtot = sum(w.shape[1] for w in ws)
    out_specs, out_shape = [], []
    for s, mode in zip(widths, modes):
        if "n" in mode:
            out_specs.append(pl.BlockSpec((tm, s), lambda i: (i, 0)))
            out_shape.append(jax.ShapeDtypeStruct((n, s), F32))
        if "t" in mode:
            per_seq = seq // tm
            out_specs.append(pl.BlockSpec((None, s, tm), lambda i: (i // per_seq, 0, i % per_seq)))
            out_shape.append(jax.ShapeDtypeStruct((n // seq, s, seq), F32))
    n_out = sum(s * len(mode) for s, mode in zip(widths, modes))
    vmem = (2 * tm * D_MODEL * 4 + D_MODEL * wtot * 2 + 3 * tm * n_out * 4) // MIB + 4
    return pl.pallas_call(
        kern, grid=(n // tm,),
        in_specs=([pl.BlockSpec((tm, D_MODEL), lambda i: (i, 0)), _resident((1, D_MODEL))]
                  + [_resident(w.shape) for w in ws]),
        out_specs=out_specs, out_shape=out_shape,
        compiler_params=_params(("arbitrary",), vmem), name="norm_proj")(x, g, *ws)


def _ssd_kernel(z_ref, xbc_ref, dt_ref, conv0_ref, h0_ref, cw_ref, cb_ref, dtb_ref, alog_ref, dsk_ref, ng_ref,
                e16_ref, y_ref, hout_ref, convout_ref, xbuf, h_scr, y_scr, *, chunk, n_chunks, t_valid):
    c = pl.program_id(1)
    L = chunk
    P0 = SSD_PAD - (SSM_CONV - 1)
    last_valid = t_valid - (n_chunks - 1) * L

    @pl.when(c == 0)
    def _():
        xbuf[P0:SSD_PAD, :] = conv0_ref[...]
        h_scr[...] = h0_ref[...]

    xbuf[SSD_PAD:SSD_PAD + L, :] = xbc_ref[...]
    cw = cw_ref[...]
    conv = cb_ref[...] + xbuf[P0:P0 + L, :] * cw[0:1, :]
    for j in range(1, SSM_CONV):
        conv = conv + xbuf[P0 + j:P0 + j + L, :] * cw[j:j + 1, :]
    act = _silu(conv)
    xs = act[:, :SSM_INNER]
    bm_b = act[:, SSM_INNER:SSM_INNER + SSM_GROUPS * SSM_STATE].astype(BF16)
    cm_b = act[:, SSM_INNER + SSM_GROUPS * SSM_STATE:].astype(BF16)

    row = _iota((L, LANES), 0)
    lane = _iota((L, LANES), 1)
    live = (lane < SSM_HEADS) & (c * L + row < t_valid)
    dt = jnp.where(live, _softplus(dt_ref[...] + dtb_ref[...]), 0.0)
    a = -jnp.exp(alog_ref[...]) * dt
    causal = _tril(L)
    tri = jnp.where(causal, 1.0, 0.0).astype(BF16)
    ac = _dot_exact_l(tri, a)
    e16 = e16_ref[...]
    ac_x = _dot_exact_r(ac, e16)
    dt_x = _dot_exact_r(dt, e16)
    if L == LANES:
        ac_t = ac.T
    else:
        ac_t = jnp.concatenate([ac, jnp.zeros((LANES - L, LANES), F32)], axis=0).T[:, :L]
    ac_last = ac[L - 1:L, :]
    ac_last_x = ac_x[L - 1:L, :]
    to_end_x = jnp.exp(ac_last_x - ac_x)
    eac_x = jnp.exp(ac_x)
    xdt = xs * dt_x
    xdt_b = xdt.astype(BF16)
    xend_b = (xdt * to_end_x).astype(BF16)
    lane_lo = lane < SSM_HEAD_DIM
    row_lo = _iota((LANES, LANES), 0) < SSM_HEAD_DIM
    heads_per_group = SSM_HEADS // SSM_GROUPS
    for g in range(SSM_GROUPS):
        bg = bm_b[:, g * SSM_STATE:(g + 1) * SSM_STATE]
        cg = cm_b[:, g * SSM_STATE:(g + 1) * SSM_STATE]
        cb = _dot_nt(cg, bg)
        for p in range(g * heads_per_group // 2, (g + 1) * heads_per_group // 2):
            sl = slice(p * LANES, (p + 1) * LANES)
            ys = []
            for j in range(2):
                h = 2 * p + j
                seg = ac[:, h:h + 1] - ac_t[h:h + 1, :]
                m = (cb * jnp.exp(jnp.where(causal, seg, -jnp.inf))).astype(BF16)
                ys.append(_dot(m, xdt_b[:, sl]))
            y_diag = jnp.where(lane_lo, ys[0], ys[1])
            s_old = h_scr[sl, :]
            y_off = _dot_nt(cg, s_old.astype(BF16)) * eac_x[:, sl]
            dec = jnp.exp(jnp.where(row_lo, ac_last[:, 2 * p:2 * p + 1], ac_last[:, 2 * p + 1:2 * p + 2]))
            h_scr[sl, :] = s_old * dec + _dot_tn(xend_b[:, sl], bg)
            y_scr[:, sl] = y_diag + y_off + dsk_ref[:, sl] * xs[:, sl]

    y_ref[...] = _rms(y_scr[...] * _silu(z_ref[...]), ng_ref[...])

    @pl.when(c == n_chunks - 1)
    def _():
        hout_ref[...] = h_scr[...]
        convout_ref[...] = xbuf[P0 + last_valid:SSD_PAD + last_valid, :]

    tail = xbuf[P0 + L:SSD_PAD + L, :]
    xbuf[P0:SSD_PAD, :] = tail


def _per_sequence(body, nb, per_seq):
    def kern(*refs, **kw):
        for s in range(nb):
            body(*[r.at[s] if i in per_seq else r for i, r in enumerate(refs)], **kw)
    return kern


def _ssd(z, xbc, dtp, conv0, h0, cw, cb, dtb, alog, dsk, ng, e16, t_valid, chunk, nb):
    bsz, t_pad, _ = z.shape
    L = chunk
    nc = t_pad // L
    kern = functools.partial(_per_sequence(_ssd_kernel, nb, (0, 1, 2, 3, 4, 12, 13, 14, 15, 16, 17)),
                             chunk=chunk, n_chunks=nc, t_valid=t_valid)
    tok = lambda w: pl.BlockSpec((nb, L, w), lambda b, c: (b, c, 0))
    per_b = lambda s: pl.BlockSpec((nb,) + s, lambda b, c: (b, 0, 0))
    return pl.pallas_call(
        kern, grid=(bsz // nb, nc),
        in_specs=[tok(SSM_INNER), tok(SSM_CONV_CH), tok(LANES), per_b((SSM_CONV - 1, SSM_CONV_CH)),
                  per_b((SSM_INNER, SSM_STATE)), _resident(cw.shape), _resident(cb.shape), _resident(dtb.shape),
                  _resident(alog.shape), _resident(dsk.shape), _resident(ng.shape), _resident(e16.shape)],
        out_specs=[tok(SSM_INNER), per_b((SSM_INNER, SSM_STATE)), per_b((SSM_CONV - 1, SSM_CONV_CH))],
        out_shape=[jax.ShapeDtypeStruct((bsz, t_pad, SSM_INNER), F32),
                   jax.ShapeDtypeStruct((bsz, SSM_INNER, SSM_STATE), F32),
                   jax.ShapeDtypeStruct((bsz, SSM_CONV - 1, SSM_CONV_CH), F32)],
        scratch_shapes=[pltpu.VMEM((nb, SSD_PAD + L, SSM_CONV_CH), F32), pltpu.VMEM((nb, SSM_INNER, SSM_STATE), F32),
                        pltpu.VMEM((nb, L, SSM_INNER), F32)],
        compiler_params=_params(("arbitrary", "arbitrary"), 40), name="ssd")(
            z, xbc, dtp, conv0, h0, cw, cb, dtb, alog, dsk, ng, e16)


def _gla_kernel(q_ref, k_ref, v_ref, g_ref, glr_ref, s0_ref, w2_ref, gb_ref, ng_ref, o_ref, sout_ref, s_scr,
                *, step_rows, chunk, n_chunks, t_valid):
    c = pl.program_id(1)
    R, L = step_rows, chunk
    n_sub = R // L
    chunk_shift = L.bit_length() - 1
    width = GLA_HEADS * GLA_DK

    @pl.when(c == 0)
    def _():
        s_scr[...] = s0_ref[...]

    x = _dot(glr_ref[...].astype(BF16), w2_ref[...]) + gb_ref[...]
    logf = _log_sigmoid(x) * (1.0 / GLA_GATE_NORM)
    logf = jnp.where(c * R + _iota((R, width), 0) < t_valid, logf, 0.0)
    row, col = _iota((R, R), 0), _iota((R, R), 1)
    causal = ((row >> chunk_shift) == (col >> chunk_shift)) & (col <= row)
    tri = jnp.where(causal, 1.0, 0.0).astype(BF16)
    bcum = _dot_exact_l(tri, logf)
    bl = [bcum[(s + 1) * L - 1:(s + 1) * L, :] for s in range(n_sub)]
    bl_rows = jnp.concatenate([jnp.broadcast_to(b, (L, width)) for b in bl], axis=0)
    q_dec = q_ref[...] * (GLA_DK ** -0.5) * jnp.exp(bcum)
    k = k_ref[...]
    k_inv = k * jnp.exp(-bcum)
    k_end = k * jnp.exp(bl_rows - bcum)
    for h in range(GLA_HEADS):
        ks = slice(h * GLA_DK, (h + 1) * GLA_DK)
        vs = slice(h * GLA_DV, (h + 1) * GLA_DV)
        qd = q_dec[:, ks].astype(BF16)
        ke = k_end[:, ks].astype(BF16)
        vb = v_ref[:, vs].astype(BF16)
        att = jnp.where(causal, _dot_nt(qd, k_inv[:, ks].astype(BF16)), 0.0)
        o = _dot(att.astype(BF16), vb)
        state = s_scr[ks, :]
        inter = []
        for s in range(n_sub):
            rs = slice(s * L, (s + 1) * L)
            inter.append(_dot(qd[rs], state.astype(BF16)))
            dcol = jnp.broadcast_to(jnp.exp(bl[s][:, ks]), (GLA_DK, GLA_DK)).T
            state = state * jnp.concatenate([dcol, dcol], axis=1) + _dot_tn(ke[rs], vb[rs])
        s_scr[ks, :] = state
        o = o + jnp.concatenate(inter, axis=0)
        o_ref[:, vs] = _rms(o, ng_ref[...]) * _silu(g_ref[:, vs])

    @pl.when(c == n_chunks - 1)
    def _():
        sout_ref[...] = s_scr[...]


def _gla(q, k, v, g, glr, s0, w2, gb, ng, t_valid, step_rows, chunk, nb):
    bsz, t_pad, _ = q.shape
    L = step_rows
    nc = t_pad // L
    kern = functools.partial(_per_sequence(_gla_kernel, nb, (0, 1, 2, 3, 4, 5, 9, 10, 11)),
                             step_rows=step_rows, chunk=chunk, n_chunks=nc, t_valid=t_valid)
    tok = lambda w: pl.BlockSpec((nb, L, w), lambda b, c: (b, c, 0))
    st = pl.BlockSpec((nb, GLA_HEADS * GLA_DK, GLA_DV), lambda b, c: (b, 0, 0))
    return pl.pallas_call(
        kern, grid=(bsz // nb, nc),
        in_specs=[tok(GLA_HEADS * GLA_DK), tok(GLA_HEADS * GLA_DK), tok(GLA_HEADS * GLA_DV), tok(GLA_HEADS * GLA_DV),
                  tok(LANES), st, _resident(w2.shape), _resident(gb.shape), _resident(ng.shape)],
        out_specs=[tok(GLA_HEADS * GLA_DV), st],
        out_shape=[jax.ShapeDtypeStruct((bsz, t_pad, GLA_HEADS * GLA_DV), F32),
                   jax.ShapeDtypeStruct((bsz, GLA_HEADS * GLA_DK, GLA_DV), F32)],
        scratch_shapes=[pltpu.VMEM((nb, GLA_HEADS * GLA_DK, GLA_DV), F32)],
        compiler_params=_params(("arbitrary", "arbitrary"), 32), name="gla")(q, k, v, g, glr, s0, w2, gb, ng)


def _proj_mlp_kernel(x_ref, a_ref, b_ref, wo_ref, g_ref, wu_ref, wd_ref, gf_ref, o_ref, *, wa, final_norm):
    x = (x_ref[...] + _dot(a_ref[...].astype(BF16), wo_ref[:wa, :])
         + _dot(b_ref[...].astype(BF16), wo_ref[wa:, :]))
    o_ref[...] = x
    xb = _rms(o_ref[...], g_ref[...]).astype(BF16)
    for c in range(D_FF // FF_CHUNK):
        sl = slice(c * FF_CHUNK, (c + 1) * FF_CHUNK)
        h = jnp.maximum(_dot(xb, wu_ref[:, sl]), 0.0)
        o_ref[...] += _dot((h * h).astype(BF16), wd_ref[sl, :])
    if final_norm:
        o_ref[...] = _rms(o_ref[...], gf_ref[...])


def _proj_mlp(x, a, b, wo, g, wu, wd, gf, tm, final_norm):
    n = x.shape[0]
    wa, wb = a.shape[1], b.shape[1]
    row = lambda s: pl.BlockSpec((tm, s), lambda i: (i, 0))
    weights = (wo.size + wu.size + wd.size) * 2
    tiles = 2 * tm * (2 * D_MODEL + wa + wb) * 4 + tm * (2 * D_MODEL + 2 * FF_CHUNK) * 4
    return pl.pallas_call(
        functools.partial(_proj_mlp_kernel, wa=wa, final_norm=final_norm), grid=(n // tm,),
        in_specs=[row(D_MODEL), row(wa), row(wb), _resident(wo.shape), _resident(g.shape), _resident(wu.shape),
                  _resident(wd.shape), _resident(gf.shape)],
        out_specs=row(D_MODEL), out_shape=jax.ShapeDtypeStruct((n, D_MODEL), F32),
        compiler_params=_params(("arbitrary",), (weights + tiles) // MIB + 6), name="proj_mlp")(
            x, a, b, wo, g, wu, wd, gf)


def _fox_gate_kernel(f_ref, bf_ref, lft_ref, ct_ref, carry, *, rows):
    @pl.when(pl.program_id(1) == 0)
    def _():
        carry[...] = jnp.zeros_like(carry)

    L = GATE_CHUNK
    tri = jnp.where(_tril(L), 1.0, 0.0).astype(BF16)
    run = carry[...]
    for r0 in range(0, rows, L):
        lf = _log_sigmoid(f_ref[r0:r0 + L, :] + bf_ref[...])
        cblk = run + _dot_exact_l(tri, lf)
        run = cblk[L - 1:L, :]
        lft_ref[:, r0:r0 + L] = lf.T[:FOX_HEADS, :]
        ct_ref[:, r0:r0 + L] = cblk.T[:FOX_HEADS, :]
    carry[...] = run


def _fox_gate(f, bf, rows):
    bsz, t_pad, _ = f.shape
    L = rows
    return pl.pallas_call(
        functools.partial(_fox_gate_kernel, rows=rows), grid=(bsz, t_pad // L),
        in_specs=[pl.BlockSpec((None, L, LANES), lambda b, t: (b, t, 0)), _resident(bf.shape)],
        out_specs=[pl.BlockSpec((None, FOX_HEADS, L), lambda b, t: (b, 0, t)),
                   pl.BlockSpec((None, FOX_HEADS, L), lambda b, t: (b, 0, t))],
        out_shape=[jax.ShapeDtypeStruct((bsz, FOX_HEADS, t_pad), F32),
                   jax.ShapeDtypeStruct((bsz, FOX_HEADS, t_pad), F32)],
        scratch_shapes=[pltpu.VMEM((1, LANES), F32)],
        compiler_params=_params(("arbitrary", "arbitrary"), 16), name="fox_gate")(f, bf)


def _fox_prompt_kernel(q_ref, k_ref, vt_ref, ct_ref, o_ref, kb_scr, vtb_scr, ck_scr, *score_scr, n_q_blocks):
    hp = pl.program_id(1)
    tq, tk = FOX_TQ, FOX_TK
    dh = FOX_HEAD_DIM
    heads = [2 * hp, 2 * hp + 1]

    kb_scr[...] = k_ref[...].astype(BF16)
    vtb_scr[...] = vt_ref[...].astype(BF16)
    for j in range(2):
        c_row = ct_ref[pl.ds(heads[j], 1), :] * LOG2E
        ck_scr[j] = jnp.broadcast_to(c_row, (LANES, c_row.shape[1])).T

    lane = _iota((tq, LANES), 1)
    kpq = tq // tk
    causal = [_iota((tk, tq), 0) + d * tk <= _iota((tk, tq), 1) for d in range(kpq)]

    def fold8(x, op):
        parts = [x[r * 8:(r + 1) * 8, :] for r in range(x.shape[0] // 8)]
        while len(parts) > 1:
            parts = [op(parts[i], parts[i + 1]) for i in range(0, len(parts), 2)]
        return parts[0]

    st = [dict() for _ in range(n_q_blocks)]
    bufs = lambda v, j: (score_scr[2 * (v % 2) + j], score_scr[FOX_SLOTS + 2 * (v % 2) + j])

    def p1_block(v, ki):
        if ki == 0:
            q = q_ref[v * tq:(v + 1) * tq, :] * (dh ** -0.5 * LOG2E)
            st[v]["qm"] = [jnp.where(lane < dh, q, 0.0).astype(BF16), jnp.where(lane >= dh, q, 0.0).astype(BF16)]
            st[v]["mx"] = [jnp.full((8, tq), -jnp.inf, F32)] * 2
        rows = slice(ki * tk, (ki + 1) * tk)
        for j in range(2):
            s = _dot_nt(kb_scr[rows, :], st[v]["qm"][j]) - jnp.concatenate([ck_scr[j, rows, :]] * (tq // LANES), axis=1)
            if ki >= v * kpq:
                s = jnp.where(causal[ki - v * kpq], s, -jnp.inf)
            bufs(v, j)[0][rows, :] = s
            st[v]["mx"][j] = jnp.maximum(st[v]["mx"][j], fold8(s, jnp.maximum))

    def p1_finish(v):
        cq = [ct_ref[pl.ds(heads[j], 1), v * tq:(v + 1) * tq] * LOG2E for j in range(2)]
        m_new = [jnp.max(st[v]["mx"][j], axis=0, keepdims=True) + cq[j] for j in range(2)]
        st[v]["shift"] = [m_new[j] - cq[j] for j in range(2)]
        st[v]["lsum"] = [jnp.zeros((8, tq), F32)] * 2

    def p2_block(v, ki):
        rows = slice(ki * tk, (ki + 1) * tk)
        for j in range(2):
            s_scr, p_scr = bufs(v, j)
            p = jnp.exp2(s_scr[rows, :] - st[v]["shift"][j])
            st[v]["lsum"][j] = st[v]["lsum"][j] + fold8(p, jnp.add)
            p_scr[rows, :] = p.astype(BF16)

    def p2_finish(v):
        n_k = (v + 1) * tq
        outs = []
        for j in range(2):
            acc = _dot(vtb_scr[j * dh:(j + 1) * dh, 0:n_k], bufs(v, j)[1][0:n_k, :])
            outs.append(acc / jnp.sum(st[v]["lsum"][j], axis=0, keepdims=True))
        o_ref[v * tq:(v + 1) * tq, :] = jnp.concatenate(outs, axis=0).T

    def stage(block, finish, v):
        return [functools.partial(block, v, ki) for ki in range((v + 1) * kpq)] + [functools.partial(finish, v)]

    for step in stage(p1_block, p1_finish, 0):
        step()
    for v in range(n_q_blocks):
        ahead = stage(p1_block, p1_finish, v + 1) if v + 1 < n_q_blocks else []
        behind = stage(p2_block, p2_finish, v)
        for i in range(max(len(ahead), len(behind))):
            for steps in (ahead, behind):
                if i < len(steps):
                    steps[i]()


def _fox_prompt(q, k, vt, ct):
    bsz, T, _ = q.shape
    tq = FOX_TQ
    return pl.pallas_call(
        functools.partial(_fox_prompt_kernel, n_q_blocks=T // tq), grid=(bsz, FOX_HEADS // 2),
        in_specs=[pl.BlockSpec((None, T, LANES), lambda b, hp: (b, 0, hp)),
                  pl.BlockSpec((None, T, LANES), lambda b, hp: (b, 0, hp)),
                  pl.BlockSpec((None, LANES, T), lambda b, hp: (b, hp, 0)),
                  pl.BlockSpec((None, FOX_HEADS, T), lambda b, hp: (b, 0, 0))],
        out_specs=pl.BlockSpec((None, T, LANES), lambda b, hp: (b, 0, hp)),
        out_shape=jax.ShapeDtypeStruct((bsz, T, FOX_HEADS * FOX_HEAD_DIM), F32),
        scratch_shapes=([pltpu.VMEM((T, LANES), BF16), pltpu.VMEM((LANES, T), BF16), pltpu.VMEM((2, T, LANES), F32)]
                        + [pltpu.VMEM((T, tq), F32)] * FOX_SLOTS + [pltpu.VMEM((T, tq), BF16)] * FOX_SLOTS),
        compiler_params=_params(("arbitrary",) * 2, 48), name="fox_prompt")(q, k, vt, ct)


def _fox_sample_kernel(pt_ref, q_ref, kn_ref, vn_ref, cnt_ref, *refs, n_steps):
    G = PAGES_PER_STEP
    k_refs, v_refs, lf_refs = refs[:G], refs[G:2 * G], refs[2 * G:3 * G]
    o_ref, qbd, kt_scr, vt_scr, m_scr, l_scr, acc_scr, carry, cncol = refs[3 * G:]
    p = pl.program_id(1)
    R = LANES
    T = R // FOX_HEADS
    width = FOX_HEADS * FOX_HEAD_DIM
    dh_shift = FOX_HEAD_DIM.bit_length() - 1
    t_shift = T.bit_length() - 1
    row = _iota((R, LANES), 0)
    lane = _iota((R, LANES), 1)

    def expand_heads(xt):
        return jnp.concatenate([jnp.broadcast_to(xt[h:h + 1, :], (T, xt.shape[1])) for h in range(FOX_HEADS)], axis=0)

    def update(s, vb):
        m = m_scr[...]
        cq = cncol[...]
        m_new = jnp.maximum(m, jnp.max(s, axis=-1, keepdims=True) + cq)
        pr = jnp.exp(s - (m_new - cq))
        alpha = jnp.exp(m - m_new)
        m_scr[...] = m_new
        l_scr[...] = alpha * l_scr[...] + jnp.sum(pr, axis=-1, keepdims=True)
        acc_scr[...] = alpha * acc_scr[...] + _dot(pr.astype(BF16), vb)

    @pl.when(p == 0)
    def _():
        q = q_ref[...] * (FOX_HEAD_DIM ** -0.5)
        qt = jnp.concatenate([q] * FOX_HEADS, axis=0)
        own = (_iota((R, width), 1) >> dh_shift) == (_iota((R, width), 0) >> t_shift)
        qbd[...] = jnp.where(own, qt, 0.0).astype(BF16)
        m_scr[...] = jnp.full_like(m_scr, -jnp.inf)
        l_scr[...] = jnp.zeros_like(l_scr)
        acc_scr[...] = jnp.zeros_like(acc_scr)
        carry[...] = jnp.zeros_like(carry)
        cn = expand_heads(cnt_ref[...])
        t_of_row = row & (T - 1)
        cncol[...] = jnp.sum(jnp.where(lane == t_of_row, cn, 0.0), axis=-1, keepdims=True)
        pad = jnp.zeros((LANES - T, width), F32)
        kb = jnp.concatenate([kn_ref[...], pad], axis=0).astype(BF16)
        vb = jnp.concatenate([vn_ref[...], pad], axis=0).astype(BF16)
        s = _dot_nt(qbd[...], kb) - cn
        update(jnp.where(lane <= t_of_row, s, -jnp.inf), vb)

    for i in range(G):
        kt_scr[:, i * PAGE_SIZE:(i + 1) * PAGE_SIZE] = k_refs[i][...].astype(BF16)
        vt_scr[:, i * PAGE_SIZE:(i + 1) * PAGE_SIZE] = v_refs[i][...].astype(BF16)
    later = jnp.where(_iota((LANES, LANES), 0) > _iota((LANES, LANES), 1), 1.0, 0.0).astype(BF16)
    lf_all = jnp.concatenate([lf_refs[i][...] for i in range(G)], axis=0)
    within = _dot_exact_r(lf_all, later)
    total = jnp.sum(lf_all, axis=-1, keepdims=True)
    run = carry[...]
    sufs = []
    for i in range(G):
        sufs.append(run + within[i * FOX_HEADS:(i + 1) * FOX_HEADS, :])
        run = run + total[i * FOX_HEADS:(i + 1) * FOX_HEADS, :]
    carry[...] = run
    s = _dot(qbd[...], kt_scr[...]) + expand_heads(jnp.concatenate(sufs, axis=1))
    m = m_scr[...]
    cq = cncol[...]
    m_new = jnp.maximum(m, jnp.max(s, axis=-1, keepdims=True) + cq)
    pr = jnp.exp(s - (m_new - cq))
    alpha = jnp.exp(m - m_new)
    m_scr[...] = m_new
    l_scr[...] = alpha * l_scr[...] + jnp.sum(pr, axis=-1, keepdims=True)
    acc_scr[...] = alpha * acc_scr[...] + _dot_nt(pr.astype(BF16), vt_scr[...])

    @pl.when(p == n_steps - 1)
    def _():
        an = acc_scr[...] / l_scr[...]
        out = jnp.zeros((T, width), F32)
        col_head = _iota((T, width), 1) >> dh_shift
        for h in range(FOX_HEADS):
            out = jnp.where(col_head == h, an[h * T:(h + 1) * T, :], out)
        o_ref[...] = out


def _fox_sample(page_table, q, kn, vn, cnt, cache_k, cache_v, lft):
    bsz, T, width = q.shape
    n_pages = page_table.shape[1]
    G = PAGES_PER_STEP
    n_steps = n_pages // G

    def page(i):
        return lambda b, p, pt: (pt[b, n_pages - 1 - (p * G + i)], 0, 0)

    per_b = lambda s: pl.BlockSpec((None,) + s, lambda b, p, pt: (b, 0, 0))
    in_specs = ([per_b((T, width)), per_b((T, width)), per_b((T, width)), per_b((FOX_HEADS, LANES))]
                + [pl.BlockSpec((None, width, PAGE_SIZE), page(i)) for i in range(G)]
                + [pl.BlockSpec((None, width, PAGE_SIZE), page(i)) for i in range(G)]
                + [pl.BlockSpec((None, FOX_HEADS, PAGE_SIZE), page(i)) for i in range(G)])
    grid_spec = pltpu.PrefetchScalarGridSpec(
        num_scalar_prefetch=1, grid=(bsz, n_steps), in_specs=in_specs, out_specs=per_b((T, width)),
        scratch_shapes=[pltpu.VMEM((LANES, width), BF16), pltpu.VMEM((width, G * PAGE_SIZE), BF16),
                        pltpu.VMEM((width, G * PAGE_SIZE), BF16), pltpu.VMEM((LANES, 1), F32),
                        pltpu.VMEM((LANES, 1), F32), pltpu.VMEM((LANES, width), F32), pltpu.VMEM((FOX_HEADS, 1), F32),
                        pltpu.VMEM((LANES, 1), F32)])
    return pl.pallas_call(
        functools.partial(_fox_sample_kernel, n_steps=n_steps), grid_spec=grid_spec,
        out_shape=jax.ShapeDtypeStruct((bsz, T, width), F32),
        compiler_params=_params(("arbitrary", "arbitrary"), 56), name="fox_sample")(
            page_table, q, kn, vn, cnt, *([cache_k] * G), *([cache_v] * G), *([lft] * G))


def _conf_kernel(u_ref, st0_ref, cw_ref, cb_ref, lg_ref, lb_ref, c_ref, stout_ref, buf, sh, *, chunk, n_chunks):
    t = pl.program_id(1)
    L = chunk
    P0 = CONF_PAD - (CONF_WIDTH - 1)

    @pl.when(t == 0)
    def _():
        buf[P0:CONF_PAD, :] = st0_ref[...]

    u = u_ref[...]
    buf[CONF_PAD:CONF_PAD + L, :] = u[:, :CONF_CH] * _sigmoid(u[:, CONF_CH:])
    span = L + CONF_PAD - SUBLANES
    for b in range(1, SUBLANES):
        sh[b - 1, 0:span, :] = buf[b:b + span, :]
    cw = cw_ref[...]
    rb = min(L, CONF_ROWS)
    for r0 in range(0, L, rb):
        acc = jnp.broadcast_to(cb_ref[...], (rb, CONF_CH))
        for j in range(CONF_WIDTH):
            a, b = divmod(P0 + j, SUBLANES)
            lo = r0 + a * SUBLANES
            src = buf[lo:lo + rb, :] if b == 0 else sh[b - 1, lo:lo + rb, :]
            acc = acc + src * cw[j:j + 1, :]
        xc = acc - jnp.mean(acc, axis=-1, keepdims=True)
        var = jnp.mean(xc * xc, axis=-1, keepdims=True)
        c_ref[r0:r0 + rb, :] = _silu(xc * lax.rsqrt(var + EPS) * lg_ref[...] + lb_ref[...])

    @pl.when(t == n_chunks - 1)
    def _():
        stout_ref[...] = buf[P0 + L:CONF_PAD + L, :]

    tail = buf[L:L + CONF_PAD, :]
    buf[0:CONF_PAD, :] = tail


def _conf(u, st0, cw, cb, lg, lb, chunk):
    bsz, T, _ = u.shape
    nc = T // chunk
    st = pl.BlockSpec((None, CONF_WIDTH - 1, CONF_CH), lambda b, t: (b, 0, 0))
    return pl.pallas_call(
        functools.partial(_conf_kernel, chunk=chunk, n_chunks=nc), grid=(bsz, nc),
        in_specs=[pl.BlockSpec((None, chunk, 2 * CONF_CH), lambda b, t: (b, t, 0)), st, _resident(cw.shape),
                  _resident(cb.shape), _resident(lg.shape), _resident(lb.shape)],
        out_specs=[pl.BlockSpec((None, chunk, CONF_CH), lambda b, t: (b, t, 0)), st],
        out_shape=[jax.ShapeDtypeStruct((bsz, T, CONF_CH), F32),
                   jax.ShapeDtypeStruct((bsz, CONF_WIDTH - 1, CONF_CH), F32)],
        scratch_shapes=[pltpu.VMEM((CONF_PAD + chunk, CONF_CH), F32),
                        pltpu.VMEM((SUBLANES - 1, CONF_PAD + chunk - SUBLANES, CONF_CH), F32)],
        compiler_params=_params(("arbitrary", "arbitrary"), 24), name="conf_conv")(u, st0, cw, cb, lg, lb)


def _pad_cols(w, width):
    return jnp.pad(w, ((0, 0), (0, width - w.shape[1])))


def _split_cols(w, sizes):
    out, off = [], 0
    for s in sizes:
        out.append(w[:, off:off + s])
        off += s
    return out


def _pad_t(a, bsz, t, t_pad):
    a = a.reshape(bsz, t, a.shape[-1])
    return a if t_pad == t else jnp.pad(a, ((0, 0), (0, t_pad - t), (0, 0)))


def _ceil_to(x, m):
    return -(-x // m) * m


def kernel(x_prompt, x_sample, cache_fox_k, cache_fox_v, cache_fox_logf, page_table, state_ssm, state_ssm_conv, state_gla, state_conf_conv, g_mix, g_mlp, g_final, w_in_ab, ssm_conv_w, ssm_conv_b, ssm_dt_bias, ssm_a_log, ssm_d, ssm_norm_g, gla_gate_w2, gla_gate_b, gla_norm_g, w_out_ab, w_in_cd, fox_b_f, conf_conv_w, conf_conv_b, conf_ln_g, conf_ln_b, w_out_cd, w_mlp_up, w_mlp_down):
    row = lambda v: v.reshape(1, -1)
    wz, wxbc, wdt, wq, wk, wv, wg, wlr = _split_cols(w_in_ab[0], AB_SPLIT)
    n_zx, n_qg = wz.shape[1] + wxbc.shape[1], sum(w.shape[1] for w in (wq, wk, wv, wg))
    off_qg = n_zx + wdt.shape[1]
    w_ab = [w_in_ab[0][:, :n_zx].astype(BF16), w_in_ab[0][:, off_qg:off_qg + n_qg].astype(BF16),
            jnp.concatenate([_pad_cols(wdt, LANES), _pad_cols(wlr, LANES)], axis=1).astype(BF16)]
    cq, ck, cv, cf, cu = _split_cols(w_in_cd[0], CD_SPLIT)
    n_qkv = cq.shape[1] + ck.shape[1] + cv.shape[1]
    w_cd = [w_in_cd[0][:, :n_qkv].astype(BF16), cu.astype(BF16), _pad_cols(cf, LANES).astype(BF16)]
    w_oab = w_out_ab[0].astype(BF16)
    w_ocd = w_out_cd[0].astype(BF16)
    w_up = w_mlp_up.astype(BF16)
    w_dn = w_mlp_down.astype(BF16)
    dtb = _pad_cols(row(ssm_dt_bias[0]), LANES)
    alog = _pad_cols(row(ssm_a_log[0]), LANES)
    dsk = row(jnp.repeat(ssm_d[0], SSM_HEAD_DIM))
    e16 = (jnp.arange(LANES)[:, None] == jnp.arange(SSM_INNER)[None, :] // SSM_HEAD_DIM).astype(BF16)
    w2 = jnp.pad(gla_gate_w2[0], ((0, LANES - GLA_RANK), (0, 0))).astype(BF16)
    bfp = _pad_cols(row(fox_b_f[0]), LANES)
    width = FOX_HEADS * FOX_HEAD_DIM
    n_pool = cache_fox_k.shape[1]
    cache_k = jnp.transpose(cache_fox_k[0], (0, 2, 3, 1)).reshape(n_pool, width, PAGE_SIZE)
    cache_v = jnp.transpose(cache_fox_v[0], (0, 2, 3, 1)).reshape(n_pool, width, PAGE_SIZE)
    lft = jnp.swapaxes(cache_fox_logf[0], 1, 2)

    def trunk(x3, sample):
        bsz, T, _ = x3.shape
        n = bsz * T
        tm = min(n, 512)
        tm_big = min(n, 512)
        x = x3.reshape(n, D_MODEL)
        if sample:
            h0 = state_ssm[0].reshape(bsz, SSM_INNER, SSM_STATE)
            conv0, s0, conf0 = state_ssm_conv[0], state_gla[0].reshape(bsz, GLA_HEADS * GLA_DK, GLA_DV), state_conf_conv[0]
        else:
            h0 = jnp.zeros((bsz, SSM_INNER, SSM_STATE), F32)
            conv0 = jnp.zeros((bsz, SSM_CONV - 1, SSM_CONV_CH), F32)
            s0 = jnp.zeros((bsz, GLA_HEADS * GLA_DK, GLA_DV), F32)
            conf0 = jnp.zeros((bsz, CONF_WIDTH - 1, CONF_CH), F32)

        z, xbc, q, k, v, g, dtp, glr = _norm_proj(x, row(g_mix[0]), w_ab, AB_SECTIONS, tm)
        short = T <= SHORT_CHUNK
        ssd_chunk = SHORT_CHUNK if short else SSD_CHUNK
        gla_rows, gla_chunk = (SHORT_CHUNK, SHORT_CHUNK) if short else (GLA_STEP, GLA_CHUNK)
        nb = SHORT_SEQS if short and bsz % SHORT_SEQS == 0 else 1
        ta = _ceil_to(T, ssd_chunk)
        y, h_new, conv_new = _ssd(_pad_t(z, bsz, T, ta), _pad_t(xbc, bsz, T, ta), _pad_t(dtp, bsz, T, ta), conv0, h0,
                                  ssm_conv_w[0], row(ssm_conv_b[0]), dtb, alog, dsk, row(ssm_norm_g[0]), e16, T,
                                  ssd_chunk, nb)
        tb = _ceil_to(T, gla_rows)
        o, s_new = _gla(_pad_t(q, bsz, T, tb), _pad_t(k, bsz, T, tb), _pad_t(v, bsz, T, tb), _pad_t(g, bsz, T, tb),
                        _pad_t(glr, bsz, T, tb), s0, w2, row(gla_gate_b[0]), row(gla_norm_g[0]), T, gla_rows,
                        gla_chunk, nb)
        y = y[:, :T].reshape(n, SSM_INNER)
        o = o[:, :T].reshape(n, GLA_HEADS * GLA_DV)
        x = _proj_mlp(x, y, o, w_oab, row(g_mlp[0]), w_up[0], w_dn[0], row(g_final), tm_big, False)

        tg = _ceil_to(T, GATE_CHUNK)
        gate_rows = min(tg, GATE_ROWS)
        head_shape = (1, bsz, T, FOX_HEADS, FOX_HEAD_DIM)
        if sample:
            q, k, v, u, f = _norm_proj(x, row(g_mix[1]), w_cd, CD_SECTIONS, tm)
            lf_t, ct = _fox_gate(_pad_t(f, bsz, T, tg), bfp, gate_rows)
            q3, k3, v3 = (a.reshape(bsz, T, width) for a in (q, k, v))
            att = _fox_sample(page_table, q3, k3, v3, ct, cache_k, cache_v, lft)
            k_out, v_out = k.reshape(head_shape), v.reshape(head_shape)
        else:
            q, k, kt, vt, u, f = _norm_proj(x, row(g_mix[1]), w_cd, CD_SECTIONS, tm, ("n", "nt", "t", "n", "n"), T)
            lf_t, ct = _fox_gate(_pad_t(f, bsz, T, tg), bfp, gate_rows)
            att = _fox_prompt(q.reshape(bsz, T, width), k.reshape(bsz, T, width), vt, ct)
            to_rows = lambda a: jnp.transpose(a.reshape(1, bsz, FOX_HEADS, FOX_HEAD_DIM, T), (0, 1, 4, 2, 3))
            k_out, v_out = to_rows(kt), to_rows(vt)
        lf = jnp.swapaxes(lf_t, 1, 2)[:, :T]
        cmod, conf_new = _conf(u.reshape(bsz, T, 2 * CONF_CH), conf0, conf_conv_w[0], row(conf_conv_b[0]),
                               row(conf_ln_g[0]), row(conf_ln_b[0]), min(T, CONF_CHUNK))
        x = _proj_mlp(x, att.reshape(n, width), cmod.reshape(n, CONF_CH), w_ocd, row(g_mlp[1]), w_up[1], w_dn[1],
                      row(g_final), tm_big, True)
        return (x.reshape(bsz, T, D_MODEL), h_new.reshape(1, bsz, SSM_HEADS, SSM_HEAD_DIM, SSM_STATE), conv_new[None],
                s_new.reshape(1, bsz, GLA_HEADS, GLA_DK, GLA_DV), k_out, v_out, lf[None], conf_new[None])

    yp, *rest_p = trunk(x_prompt, False)
    ys, *rest_s = trunk(x_sample, True)
    return (yp, ys, *rest_p, *rest_s)
```

```python
import functools

import jax
import jax.numpy as jnp
from jax import lax
from jax.experimental import pallas as pl
from jax.experimental.pallas import tpu as pltpu

F32 = jnp.float32
BF16 = jnp.bfloat16
EPS = 1e-6
LOG2E = 1.4426950408889634

LANES = 128
SUBLANES = 8
MIB = 1024 * 1024

D_MODEL = 1024
D_FF = 4 * D_MODEL
SSM_HEADS = 16
SSM_HEAD_DIM = 64
SSM_INNER = SSM_HEADS * SSM_HEAD_DIM
SSM_GROUPS = 2
SSM_STATE = 128
SSM_CONV = 4
SSM_CONV_CH = SSM_INNER + 2 * SSM_GROUPS * SSM_STATE
GLA_HEADS = 4
GLA_DK = 128
GLA_DV = 256
GLA_RANK = 16
GLA_GATE_NORM = 16.0
FOX_HEADS = 16
FOX_HEAD_DIM = 64
CONF_CH = 512
CONF_WIDTH = 31
PAGE_SIZE = 128

AB_SPLIT = (SSM_INNER, SSM_CONV_CH, SSM_HEADS, GLA_HEADS * GLA_DK, GLA_HEADS * GLA_DK,
            GLA_HEADS * GLA_DV, GLA_HEADS * GLA_DV, GLA_RANK)
CD_SPLIT = (FOX_HEADS * FOX_HEAD_DIM,) * 3 + (FOX_HEADS, 2 * CONF_CH)
AB_SECTIONS = (SSM_INNER, SSM_CONV_CH, GLA_HEADS * GLA_DK, GLA_HEADS * GLA_DK, GLA_HEADS * GLA_DV,
               GLA_HEADS * GLA_DV, LANES, LANES)
CD_SECTIONS = (1024, 1024, 1024, 2 * CONF_CH, LANES)

SSD_CHUNK = 128
GLA_CHUNK = 64
GLA_STEP = 128
SHORT_CHUNK = 16
SHORT_SEQS = 4
GATE_CHUNK = 128
GATE_ROWS = 512
FOX_TQ = 256
FOX_TK = 256
FOX_SLOTS = 4
CONF_CHUNK = 256
CONF_PAD = 32
CONF_ROWS = 32
SSD_PAD = 8
FF_CHUNK = 1024
PAGES_PER_STEP = 16


def _dot(a, b):
    return jnp.dot(a, b, preferred_element_type=F32)


def _dot_nt(a, b):
    return lax.dot_general(a, b, (((1,), (1,)), ((), ())), preferred_element_type=F32)


def _dot_tn(a, b):
    return lax.dot_general(a, b, (((0,), (0,)), ((), ())), preferred_element_type=F32)


def _split3(x):
    hi = x.astype(BF16)
    r1 = x - hi.astype(F32)
    mid = r1.astype(BF16)
    lo = (r1 - mid.astype(F32)).astype(BF16)
    return hi, mid, lo


def _dot_exact_l(m, x):
    hi, mid, lo = _split3(x)
    return _dot(m, lo) + _dot(m, mid) + _dot(m, hi)


def _dot_exact_r(x, m):
    hi, mid, lo = _split3(x)
    return _dot(lo, m) + _dot(mid, m) + _dot(hi, m)


def _sigmoid(x):
    return 1.0 / (1.0 + jnp.exp2(x * (-LOG2E)))


def _silu(x):
    return x * _sigmoid(x)


def _softplus(x):
    return jnp.maximum(x, 0.0) + jnp.log1p(jnp.exp2(jnp.abs(x) * (-LOG2E)))


def _log_sigmoid(x):
    return jnp.minimum(x, 0.0) - jnp.log1p(jnp.exp2(jnp.abs(x) * (-LOG2E)))


def _rms(x, g):
    return x * lax.rsqrt(jnp.mean(x * x, axis=-1, keepdims=True) + EPS) * g


def _iota(shape, dim):
    return lax.broadcasted_iota(jnp.int32, shape, dim)


def _tril(n):
    return _iota((n, n), 1) <= _iota((n, n), 0)


def _params(semantics, vmem_mib):
    return pltpu.CompilerParams(dimension_semantics=semantics, vmem_limit_bytes=vmem_mib * MIB)


def _resident(shape):
    nd = len(shape)
    return pl.BlockSpec(shape, lambda *_: (0,) * nd, pipeline_mode=pl.Buffered(1))


def _norm_proj_kernel(x_ref, g_ref, *refs, n_w, sections):
    w_refs, out_refs = refs[:n_w], refs[n_w:]
    xb = _rms(x_ref[...], g_ref[...]).astype(BF16)
    outs = iter(out_refs)
    for wi, off, width, mode in sections:
        y = _dot(xb, w_refs[wi][:, off:off + width])
        if "n" in mode:
            next(outs)[...] = y
        if "b" in mode:
            next(outs)[...] = y.astype(BF16)
        if "t" in mode:
            next(outs)[...] = y.T
        if "s" in mode:
            next(outs)[...] = y.T.astype(BF16)


def _norm_proj(x, g, ws, widths, tm, modes=None, seq=None):
    n = x.shape[0]
    modes = modes or ("n",) * len(widths)
    where, wi, off = [], 0, 0
    for s in widths:
        if off == ws[wi].shape[1]:
            wi, off = wi + 1, 0
        where.append((wi, off))
        off += s
    assert wi == len(ws) - 1 and off == ws[wi].shape[1]
    kern = functools.partial(_norm_proj_kernel, n_w=len(ws),
                             sections=tuple((wi, off, s, m) for (wi, off), s, m in zip(where, widths, modes)))
    wtot = sum(w.shape[1] for w in ws)
    out_specs, out_shape = [], []
    for s, mode in zip(widths, modes):
        for letter, dtype in (("n", F32), ("b", BF16)):
            if letter in mode:
                out_specs.append(pl.BlockSpec((tm, s), lambda i: (i, 0)))
                out_shape.append(jax.ShapeDtypeStruct((n, s), dtype))
        for letter, dtype in (("t", F32), ("s", BF16)):
            if letter in mode:
                per_seq = seq // tm
                out_specs.append(pl.BlockSpec((None, s, tm), lambda i: (i // per_seq, 0, i % per_seq)))
                out_shape.append(jax.ShapeDtypeStruct((n // seq, s, seq), dtype))
    n_out = sum(s * len(mode) for s, mode in zip(widths, modes))
    vmem = (2 * tm * D_MODEL * 4 + D_MODEL * wtot * 2 + 3 * tm * n_out * 4) // MIB + 4
    return pl.pallas_call(
        kern, grid=(n // tm,),
        in_specs=([pl.BlockSpec((tm, D_MODEL), lambda i: (i, 0)), _resident((1, D_MODEL))]
                  + [_resident(w.shape) for w in ws]),
        out_specs=out_specs, out_shape=out_shape,
        compiler_params=_params(("arbitrary",), vmem), name="norm_proj")(x, g, *ws)


def _ssd_kernel(z_ref, xbc_ref, dt_ref, conv0_ref, h0_ref, cw_ref, cb_ref, dtb_ref, alog_ref, dsk_ref, ng_ref,
                e16_ref, y_ref, hout_ref, convout_ref, xbuf, h_scr, y_scr, *, chunk, n_chunks, t_valid):
    c = pl.program_id(1)
    L = chunk
    P0 = SSD_PAD - (SSM_CONV - 1)
    last_valid = t_valid - (n_chunks - 1) * L

    @pl.when(c == 0)
    def _():
        xbuf[P0:SSD_PAD, :] = conv0_ref[...]
        h_scr[...] = h0_ref[...]

    xbuf[SSD_PAD:SSD_PAD + L, :] = xbc_ref[...]
    cw = cw_ref[...]
    conv = cb_ref[...] + xbuf[P0:P0 + L, :] * cw[0:1, :]
    for j in range(1, SSM_CONV):
        conv = conv + xbuf[P0 + j:P0 + j + L, :] * cw[j:j + 1, :]
    act = _silu(conv)
    xs = act[:, :SSM_INNER]
    bm_b = act[:, SSM_INNER:SSM_INNER + SSM_GROUPS * SSM_STATE].astype(BF16)
    cm_b = act[:, SSM_INNER + SSM_GROUPS * SSM_STATE:].astype(BF16)

    row = _iota((L, LANES), 0)
    lane = _iota((L, LANES), 1)
    live = (lane < SSM_HEADS) & (c * L + row < t_valid)
    dt = jnp.where(live, _softplus(dt_ref[...] + dtb_ref[...]), 0.0)
    a = -jnp.exp(alog_ref[...]) * dt
    causal = _tril(L)
    tri = jnp.where(causal, 1.0, 0.0).astype(BF16)
    ac = _dot_exact_l(tri, a)
    e16 = e16_ref[...]
    ac_x = _dot_exact_r(ac, e16)
    dt_x = _dot_exact_r(dt, e16)
    if L == LANES:
        ac_t = ac.T
    else:
        ac_t = jnp.concatenate([ac, jnp.zeros((LANES - L, LANES), F32)], axis=0).T[:, :L]
    ac_last = ac[L - 1:L, :]
    ac_last_x = ac_x[L - 1:L, :]
    to_end_x = jnp.exp(ac_last_x - ac_x)
    eac_x = jnp.exp(ac_x)
    xdt = xs * dt_x
    xdt_b = xdt.astype(BF16)
    xend_b = (xdt * to_end_x).astype(BF16)
    lane_lo = lane < SSM_HEAD_DIM
    row_lo = _iota((LANES, LANES), 0) < SSM_HEAD_DIM
    heads_per_group = SSM_HEADS // SSM_GROUPS
    for g in range(SSM_GROUPS):
        bg = bm_b[:, g * SSM_STATE:(g + 1) * SSM_STATE]
        cg = cm_b[:, g * SSM_STATE:(g + 1) * SSM_STATE]
        cb = _dot_nt(cg, bg)
        for p in range(g * heads_per_group // 2, (g + 1) * heads_per_group // 2):
            sl = slice(p * LANES, (p + 1) * LANES)
            ys = []
            for j in range(2):
                h = 2 * p + j
                seg = ac[:, h:h + 1] - ac_t[h:h + 1, :]
                m = (cb * jnp.exp(jnp.where(causal, seg, -jnp.inf))).astype(BF16)
                ys.append(_dot(m, xdt_b[:, sl]))
            y_diag = jnp.where(lane_lo, ys[0], ys[1])
            s_old = h_scr[sl, :]
            y_off = _dot_nt(cg, s_old.astype(BF16)) * eac_x[:, sl]
            dec = jnp.exp(jnp.where(row_lo, ac_last[:, 2 * p:2 * p + 1], ac_last[:, 2 * p + 1:2 * p + 2]))
            h_scr[sl, :] = s_old * dec + _dot_tn(xend_b[:, sl], bg)
            y_scr[:, sl] = y_diag + y_off + dsk_ref[:, sl] * xs[:, sl]

    y_ref[...] = _rms(y_scr[...] * _silu(z_ref[...]), ng_ref[...])

    @pl.when(c == n_chunks - 1)
    def _():
        hout_ref[...] = h_scr[...]
        convout_ref[...] = xbuf[P0 + last_valid:SSD_PAD + last_valid, :]

    tail = xbuf[P0 + L:SSD_PAD + L, :]
    xbuf[P0:SSD_PAD, :] = tail


def _per_sequence(body, nb, per_seq):
    def kern(*refs, **kw):
        for s in range(nb):
            body(*[r.at[s] if i in per_seq else r for i, r in enumerate(refs)], **kw)
    return kern


def _ssd(z, xbc, dtp, conv0, h0, cw, cb, dtb, alog, dsk, ng, e16, t_valid, chunk, nb):
    bsz, t_pad, _ = z.shape
    L = chunk
    nc = t_pad // L
    kern = functools.partial(_per_sequence(_ssd_kernel, nb, (0, 1, 2, 3, 4, 12, 13, 14, 15, 16, 17)),
                             chunk=chunk, n_chunks=nc, t_valid=t_valid)
    tok = lambda w: pl.BlockSpec((nb, L, w), lambda b, c: (b, c, 0))
    per_b = lambda s: pl.BlockSpec((nb,) + s, lambda b, c: (b, 0, 0))
    return pl.pallas_call(
        kern, grid=(bsz // nb, nc),
        in_specs=[tok(SSM_INNER), tok(SSM_CONV_CH), tok(LANES), per_b((SSM_CONV - 1, SSM_CONV_CH)),
                  per_b((SSM_INNER, SSM_STATE)), _resident(cw.shape), _resident(cb.shape), _resident(dtb.shape),
                  _resident(alog.shape), _resident(dsk.shape), _resident(ng.shape), _resident(e16.shape)],
        out_specs=[tok(SSM_INNER), per_b((SSM_INNER, SSM_STATE)), per_b((SSM_CONV - 1, SSM_CONV_CH))],
        out_shape=[jax.ShapeDtypeStruct((bsz, t_pad, SSM_INNER), F32),
                   jax.ShapeDtypeStruct((bsz, SSM_INNER, SSM_STATE), F32),
                   jax.ShapeDtypeStruct((bsz, SSM_CONV - 1, SSM_CONV_CH), F32)],
        scratch_shapes=[pltpu.VMEM((nb, SSD_PAD + L, SSM_CONV_CH), F32), pltpu.VMEM((nb, SSM_INNER, SSM_STATE), F32),
                        pltpu.VMEM((nb, L, SSM_INNER), F32)],
        compiler_params=_params(("arbitrary", "arbitrary"), 40), name="ssd")(
            z, xbc, dtp, conv0, h0, cw, cb, dtb, alog, dsk, ng, e16)


def _gla_kernel(q_ref, k_ref, v_ref, g_ref, glr_ref, s0_ref, w2_ref, gb_ref, ng_ref, o_ref, sout_ref, s_scr,
                *, step_rows, chunk, n_chunks, t_valid):
    c = pl.program_id(1)
    R, L = step_rows, chunk
    n_sub = R // L
    chunk_shift = L.bit_length() - 1
    width = GLA_HEADS * GLA_DK

    @pl.when(c == 0)
    def _():
        s_scr[...] = s0_ref[...]

    x = _dot(glr_ref[...].astype(BF16), w2_ref[...]) + gb_ref[...]
    logf = _log_sigmoid(x) * (1.0 / GLA_GATE_NORM)
    logf = jnp.where(c * R + _iota((R, width), 0) < t_valid, logf, 0.0)
    row, col = _iota((R, R), 0), _iota((R, R), 1)
    causal = ((row >> chunk_shift) == (col >> chunk_shift)) & (col <= row)
    tri = jnp.where(causal, 1.0, 0.0).astype(BF16)
    bcum = _dot_exact_l(tri, logf)
    bl = [bcum[(s + 1) * L - 1:(s + 1) * L, :] for s in range(n_sub)]
    bl_rows = jnp.concatenate([jnp.broadcast_to(b, (L, width)) for b in bl], axis=0)
    q_dec = q_ref[...] * (GLA_DK ** -0.5) * jnp.exp(bcum)
    k = k_ref[...]
    k_inv = k * jnp.exp(-bcum)
    k_end = k * jnp.exp(bl_rows - bcum)
    for h in range(GLA_HEADS):
        ks = slice(h * GLA_DK, (h + 1) * GLA_DK)
        vs = slice(h * GLA_DV, (h + 1) * GLA_DV)
        qd = q_dec[:, ks].astype(BF16)
        ke = k_end[:, ks].astype(BF16)
        vb = v_ref[:, vs].astype(BF16)
        att = jnp.where(causal, _dot_nt(qd, k_inv[:, ks].astype(BF16)), 0.0)
        o = _dot(att.astype(BF16), vb)
        state = s_scr[ks, :]
        inter = []
        for s in range(n_sub):
            rs = slice(s * L, (s + 1) * L)
            inter.append(_dot(qd[rs], state.astype(BF16)))
            dcol = jnp.broadcast_to(jnp.exp(bl[s][:, ks]), (GLA_DK, GLA_DK)).T
            state = state * jnp.concatenate([dcol, dcol], axis=1) + _dot_tn(ke[rs], vb[rs])
        s_scr[ks, :] = state
        o = o + jnp.concatenate(inter, axis=0)
        o_ref[:, vs] = _rms(o, ng_ref[...]) * _silu(g_ref[:, vs])

    @pl.when(c == n_chunks - 1)
    def _():
        sout_ref[...] = s_scr[...]


def _gla(q, k, v, g, glr, s0, w2, gb, ng, t_valid, step_rows, chunk, nb):
    bsz, t_pad, _ = q.shape
    L = step_rows
    nc = t_pad // L
    kern = functools.partial(_per_sequence(_gla_kernel, nb, (0, 1, 2, 3, 4, 5, 9, 10, 11)),
                             step_rows=step_rows, chunk=chunk, n_chunks=nc, t_valid=t_valid)
    tok = lambda w: pl.BlockSpec((nb, L, w), lambda b, c: (b, c, 0))
    st = pl.BlockSpec((nb, GLA_HEADS * GLA_DK, GLA_DV), lambda b, c: (b, 0, 0))
    return pl.pallas_call(
        kern, grid=(bsz // nb, nc),
        in_specs=[tok(GLA_HEADS * GLA_DK), tok(GLA_HEADS * GLA_DK), tok(GLA_HEADS * GLA_DV), tok(GLA_HEADS * GLA_DV),
                  tok(LANES), st, _resident(w2.shape), _resident(gb.shape), _resident(ng.shape)],
        out_specs=[tok(GLA_HEADS * GLA_DV), st],
        out_shape=[jax.ShapeDtypeStruct((bsz, t_pad, GLA_HEADS * GLA_DV), F32),
                   jax.ShapeDtypeStruct((bsz, GLA_HEADS * GLA_DK, GLA_DV), F32)],
        scratch_shapes=[pltpu.VMEM((nb, GLA_HEADS * GLA_DK, GLA_DV), F32)],
        compiler_params=_params(("arbitrary", "arbitrary"), 32), name="gla")(q, k, v, g, glr, s0, w2, gb, ng)


def _proj_mlp_kernel(x_ref, a_ref, b_ref, wo_ref, g_ref, wu_ref, wd_ref, gf_ref, o_ref, *, wa, final_norm):
    x = (x_ref[...] + _dot(a_ref[...].astype(BF16), wo_ref[:wa, :])
         + _dot(b_ref[...].astype(BF16), wo_ref[wa:, :]))
    o_ref[...] = x
    xb = _rms(o_ref[...], g_ref[...]).astype(BF16)
    for c in range(D_FF // FF_CHUNK):
        sl = slice(c * FF_CHUNK, (c + 1) * FF_CHUNK)
        h = jnp.maximum(_dot(xb, wu_ref[:, sl]), 0.0)
        o_ref[...] += _dot((h * h).astype(BF16), wd_ref[sl, :])
    if final_norm:
        o_ref[...] = _rms(o_ref[...], gf_ref[...])


def _proj_mlp(x, a, b, wo, g, wu, wd, gf, tm, final_norm):
    n = x.shape[0]
    wa, wb = a.shape[1], b.shape[1]
    row = lambda s: pl.BlockSpec((tm, s), lambda i: (i, 0))
    weights = (wo.size + wu.size + wd.size) * 2
    tiles = 2 * tm * (2 * D_MODEL + wa + wb) * 4 + tm * (2 * D_MODEL + 2 * FF_CHUNK) * 4
    return pl.pallas_call(
        functools.partial(_proj_mlp_kernel, wa=wa, final_norm=final_norm), grid=(n // tm,),
        in_specs=[row(D_MODEL), row(wa), row(wb), _resident(wo.shape), _resident(g.shape), _resident(wu.shape),
                  _resident(wd.shape), _resident(gf.shape)],
        out_specs=row(D_MODEL), out_shape=jax.ShapeDtypeStruct((n, D_MODEL), F32),
        compiler_params=_params(("arbitrary",), (weights + tiles) // MIB + 6), name="proj_mlp")(
            x, a, b, wo, g, wu, wd, gf)


def _fox_gate_kernel(f_ref, bf_ref, lft_ref, ct_ref, carry, *, rows):
    @pl.when(pl.program_id(1) == 0)
    def _():
        carry[...] = jnp.zeros_like(carry)

    L = GATE_CHUNK
    tri = jnp.where(_tril(L), 1.0, 0.0).astype(BF16)
    run = carry[...]
    for r0 in range(0, rows, L):
        lf = _log_sigmoid(f_ref[r0:r0 + L, :] + bf_ref[...])
        cblk = run + _dot_exact_l(tri, lf)
        run = cblk[L - 1:L, :]
        lft_ref[:, r0:r0 + L] = lf.T[:FOX_HEADS, :]
        ct_ref[:, r0:r0 + L] = cblk.T[:FOX_HEADS, :]
    carry[...] = run


def _fox_gate(f, bf, rows):
    bsz, t_pad, _ = f.shape
    L = rows
    return pl.pallas_call(
        functools.partial(_fox_gate_kernel, rows=rows), grid=(bsz, t_pad // L),
        in_specs=[pl.BlockSpec((None, L, LANES), lambda b, t: (b, t, 0)), _resident(bf.shape)],
        out_specs=[pl.BlockSpec((None, FOX_HEADS, L), lambda b, t: (b, 0, t)),
                   pl.BlockSpec((None, FOX_HEADS, L), lambda b, t: (b, 0, t))],
        out_shape=[jax.ShapeDtypeStruct((bsz, FOX_HEADS, t_pad), F32),
                   jax.ShapeDtypeStruct((bsz, FOX_HEADS, t_pad), F32)],
        scratch_shapes=[pltpu.VMEM((1, LANES), F32)],
        compiler_params=_params(("arbitrary", "arbitrary"), 16), name="fox_gate")(f, bf)


def _fox_prompt_kernel(q_ref, kb_ref, vtb_ref, ct_ref, o_ref, ck_scr, *score_scr, n_q_blocks):
    hp = pl.program_id(1)
    tq, tk = FOX_TQ, FOX_TK
    dh = FOX_HEAD_DIM
    heads = [2 * hp, 2 * hp + 1]

    for j in range(2):
        c_row = ct_ref[pl.ds(heads[j], 1), :] * LOG2E
        ck_scr[j] = jnp.broadcast_to(c_row, (LANES, c_row.shape[1])).T

    lane = _iota((tq, LANES), 1)
    kpq = tq // tk
    causal = [_iota((tk, tq), 0) + d * tk <= _iota((tk, tq), 1) for d in range(kpq)]

    def fold8(x, op):
        parts = [x[r * 8:(r + 1) * 8, :] for r in range(x.shape[0] // 8)]
        while len(parts) > 1:
            parts = [op(parts[i], parts[i + 1]) for i in range(0, len(parts), 2)]
        return parts[0]

    st = [dict() for _ in range(n_q_blocks)]
    bufs = lambda v, j: (score_scr[2 * (v % 2) + j], score_scr[FOX_SLOTS + 2 * (v % 2) + j])

    def p1_block(v, ki):
        if ki == 0:
            q = q_ref[v * tq:(v + 1) * tq, :] * (dh ** -0.5 * LOG2E)
            st[v]["qm"] = [jnp.where(lane < dh, q, 0.0).astype(BF16), jnp.where(lane >= dh, q, 0.0).astype(BF16)]
            st[v]["mx"] = [jnp.full((8, tq), -jnp.inf, F32)] * 2
        rows = slice(ki * tk, (ki + 1) * tk)
        for j in range(2):
            s = _dot_nt(kb_ref[rows, :], st[v]["qm"][j]) - jnp.concatenate([ck_scr[j, rows, :]] * (tq // LANES), axis=1)
            if ki >= v * kpq:
                s = jnp.where(causal[ki - v * kpq], s, -jnp.inf)
            bufs(v, j)[0][rows, :] = s
            st[v]["mx"][j] = jnp.maximum(st[v]["mx"][j], fold8(s, jnp.maximum))

    def p1_finish(v):
        cq = [ct_ref[pl.ds(heads[j], 1), v * tq:(v + 1) * tq] * LOG2E for j in range(2)]
        m_new = [jnp.max(st[v]["mx"][j], axis=0, keepdims=True) + cq[j] for j in range(2)]
        st[v]["shift"] = [m_new[j] - cq[j] for j in range(2)]
        st[v]["lsum"] = [jnp.zeros((8, tq), F32)] * 2

    def p2_block(v, ki):
        rows = slice(ki * tk, (ki + 1) * tk)
        for j in range(2):
            s_scr, p_scr = bufs(v, j)
            p = jnp.exp2(s_scr[rows, :] - st[v]["shift"][j])
            st[v]["lsum"][j] = st[v]["lsum"][j] + fold8(p, jnp.add)
            p_scr[rows, :] = p.astype(BF16)

    def p2_finish(v):
        n_k = (v + 1) * tq
        outs = []
        for j in range(2):
            acc = _dot(vtb_ref[j * dh:(j + 1) * dh, 0:n_k], bufs(v, j)[1][0:n_k, :])
            outs.append(acc / jnp.sum(st[v]["lsum"][j], axis=0, keepdims=True))
        o_ref[v * tq:(v + 1) * tq, :] = jnp.concatenate(outs, axis=0).T

    def stage(block, finish, v):
        return [functools.partial(block, v, ki) for ki in range((v + 1) * kpq)] + [functools.partial(finish, v)]

    for step in stage(p1_block, p1_finish, 0):
        step()
    for v in range(n_q_blocks):
        ahead = stage(p1_block, p1_finish, v + 1) if v + 1 < n_q_blocks else []
        behind = stage(p2_block, p2_finish, v)
        for i in range(max(len(ahead), len(behind))):
            for steps in (ahead, behind):
                if i < len(steps):
                    steps[i]()


def _fox_prompt(q, k, vt, ct):
    bsz, T, _ = q.shape
    tq = FOX_TQ
    return pl.pallas_call(
        functools.partial(_fox_prompt_kernel, n_q_blocks=T // tq), grid=(bsz, FOX_HEADS // 2),
        in_specs=[pl.BlockSpec((None, T, LANES), lambda b, hp: (b, 0, hp)),
                  pl.BlockSpec((None, T, LANES), lambda b, hp: (b, 0, hp)),
                  pl.BlockSpec((None, LANES, T), lambda b, hp: (b, hp, 0)),
                  pl.BlockSpec((None, FOX_HEADS, T), lambda b, hp: (b, 0, 0))],
        out_specs=pl.BlockSpec((None, T, LANES), lambda b, hp: (b, 0, hp)),
        out_shape=jax.ShapeDtypeStruct((bsz, T, FOX_HEADS * FOX_HEAD_DIM), F32),
        scratch_shapes=([pltpu.VMEM((2, T, LANES), F32)]
                        + [pltpu.VMEM((T, tq), F32)] * FOX_SLOTS + [pltpu.VMEM((T, tq), BF16)] * FOX_SLOTS),
        compiler_params=_params(("arbitrary",) * 2, 48), name="fox_prompt")(q, k, vt, ct)


def _fox_sample_kernel(pt_ref, q_ref, kn_ref, vn_ref, cnt_ref, *refs, n_steps):
    G = PAGES_PER_STEP
    k_refs, v_refs, lf_refs = refs[:G], refs[G:2 * G], refs[2 * G:3 * G]
    o_ref, qbd, kt_scr, vt_scr, m_scr, l_scr, acc_scr, carry, cncol = refs[3 * G:]
    p = pl.program_id(1)
    R = LANES
    T = R // FOX_HEADS
    width = FOX_HEADS * FOX_HEAD_DIM
    dh_shift = FOX_HEAD_DIM.bit_length() - 1
    t_shift = T.bit_length() - 1
    row = _iota((R, LANES), 0)
    lane = _iota((R, LANES), 1)

    def expand_heads(xt):
        return jnp.concatenate([jnp.broadcast_to(xt[h:h + 1, :], (T, xt.shape[1])) for h in range(FOX_HEADS)], axis=0)

    def update(s, vb):
        m = m_scr[...]
        cq = cncol[...]
        m_new = jnp.maximum(m, jnp.max(s, axis=-1, keepdims=True) + cq)
        pr = jnp.exp(s - (m_new - cq))
        alpha = jnp.exp(m - m_new)
        m_scr[...] = m_new
        l_scr[...] = alpha * l_scr[...] + jnp.sum(pr, axis=-1, keepdims=True)
        acc_scr[...] = alpha * acc_scr[...] + _dot(pr.astype(BF16), vb)

    @pl.when(p == 0)
    def _():
        q = q_ref[...] * (FOX_HEAD_DIM ** -0.5)
        qt = jnp.concatenate([q] * FOX_HEADS, axis=0)
        own = (_iota((R, width), 1) >> dh_shift) == (_iota((R, width), 0) >> t_shift)
        qbd[...] = jnp.where(own, qt, 0.0).astype(BF16)
        m_scr[...] = jnp.full_like(m_scr, -jnp.inf)
        l_scr[...] = jnp.zeros_like(l_scr)
        acc_scr[...] = jnp.zeros_like(acc_scr)
        carry[...] = jnp.zeros_like(carry)
        cn = expand_heads(cnt_ref[...])
        t_of_row = row & (T - 1)
        cncol[...] = jnp.sum(jnp.where(lane == t_of_row, cn, 0.0), axis=-1, keepdims=True)
        pad = jnp.zeros((LANES - T, width), F32)
        kb = jnp.concatenate([kn_ref[...], pad], axis=0).astype(BF16)
        vb = jnp.concatenate([vn_ref[...], pad], axis=0).astype(BF16)
        s = _dot_nt(qbd[...], kb) - cn
        update(jnp.where(lane <= t_of_row, s, -jnp.inf), vb)

    for i in range(G):
        kt_scr[:, i * PAGE_SIZE:(i + 1) * PAGE_SIZE] = k_refs[i][...].astype(BF16)
        vt_scr[:, i * PAGE_SIZE:(i + 1) * PAGE_SIZE] = v_refs[i][...].astype(BF16)
    later = jnp.where(_iota((LANES, LANES), 0) > _iota((LANES, LANES), 1), 1.0, 0.0).astype(BF16)
    lf_all = jnp.concatenate([lf_refs[i][...] for i in range(G)], axis=0)
    within = _dot_exact_r(lf_all, later)
    total = jnp.sum(lf_all, axis=-1, keepdims=True)
    run = carry[...]
    sufs = []
    for i in range(G):
        sufs.append(run + within[i * FOX_HEADS:(i + 1) * FOX_HEADS, :])
        run = run + total[i * FOX_HEADS:(i + 1) * FOX_HEADS, :]
    carry[...] = run
    s = _dot(qbd[...], kt_scr[...]) + expand_heads(jnp.concatenate(sufs, axis=1))
    m = m_scr[...]
    cq = cncol[...]
    m_new = jnp.maximum(m, jnp.max(s, axis=-1, keepdims=True) + cq)
    pr = jnp.exp(s - (m_new - cq))
    alpha = jnp.exp(m - m_new)
    m_scr[...] = m_new
    l_scr[...] = alpha * l_scr[...] + jnp.sum(pr, axis=-1, keepdims=True)
    acc_scr[...] = alpha * acc_scr[...] + _dot_nt(pr.astype(BF16), vt_scr[...])

    @pl.when(p == n_steps - 1)
    def _():
        an = acc_scr[...] / l_scr[...]
        out = jnp.zeros((T, width), F32)
        col_head = _iota((T, width), 1) >> dh_shift
        for h in range(FOX_HEADS):
            out = jnp.where(col_head == h, an[h * T:(h + 1) * T, :], out)
        o_ref[...] = out


def _fox_sample(page_table, q, kn, vn, cnt, cache_k, cache_v, lft):
    bsz, T, width = q.shape
    n_pages = page_table.shape[1]
    G = PAGES_PER_STEP
    n_steps = n_pages // G

    def page(i):
        return lambda b, p, pt: (pt[b, n_pages - 1 - (p * G + i)], 0, 0)

    per_b = lambda s: pl.BlockSpec((None,) + s, lambda b, p, pt: (b, 0, 0))
    in_specs = ([per_b((T, width)), per_b((T, width)), per_b((T, width)), per_b((FOX_HEADS, LANES))]
                + [pl.BlockSpec((None, width, PAGE_SIZE), page(i)) for i in range(G)]
                + [pl.BlockSpec((None, width, PAGE_SIZE), page(i)) for i in range(G)]
                + [pl.BlockSpec((None, FOX_HEADS, PAGE_SIZE), page(i)) for i in range(G)])
    grid_spec = pltpu.PrefetchScalarGridSpec(
        num_scalar_prefetch=1, grid=(bsz, n_steps), in_specs=in_specs, out_specs=per_b((T, width)),
        scratch_shapes=[pltpu.VMEM((LANES, width), BF16), pltpu.VMEM((width, G * PAGE_SIZE), BF16),
                        pltpu.VMEM((width, G * PAGE_SIZE), BF16), pltpu.VMEM((LANES, 1), F32),
                        pltpu.VMEM((LANES, 1), F32), pltpu.VMEM((LANES, width), F32), pltpu.VMEM((FOX_HEADS, 1), F32),
                        pltpu.VMEM((LANES, 1), F32)])
    return pl.pallas_call(
        functools.partial(_fox_sample_kernel, n_steps=n_steps), grid_spec=grid_spec,
        out_shape=jax.ShapeDtypeStruct((bsz, T, width), F32),
        compiler_params=_params(("arbitrary", "arbitrary"), 56), name="fox_sample")(
            page_table, q, kn, vn, cnt, *([cache_k] * G), *([cache_v] * G), *([lft] * G))


def _conf_kernel(u_ref, st0_ref, cw_ref, cb_ref, lg_ref, lb_ref, c_ref, stout_ref, buf, sh, *, chunk, n_chunks):
    t = pl.program_id(1)
    L = chunk
    P0 = CONF_PAD - (CONF_WIDTH - 1)

    @pl.when(t == 0)
    def _():
        buf[P0:CONF_PAD, :] = st0_ref[...]

    u = u_ref[...]
    buf[CONF_PAD:CONF_PAD + L, :] = u[:, :CONF_CH] * _sigmoid(u[:, CONF_CH:])
    span = L + CONF_PAD - SUBLANES
    for b in range(1, SUBLANES):
        sh[b - 1, 0:span, :] = buf[b:b + span, :]
    cw = cw_ref[...]
    rb = min(L, CONF_ROWS)
    for r0 in range(0, L, rb):
        acc = jnp.broadcast_to(cb_ref[...], (rb, CONF_CH))
        for j in range(CONF_WIDTH):
            a, b = divmod(P0 + j, SUBLANES)
            lo = r0 + a * SUBLANES
            src = buf[lo:lo + rb, :] if b == 0 else sh[b - 1, lo:lo + rb, :]
            acc = acc + src * cw[j:j + 1, :]
        xc = acc - jnp.mean(acc, axis=-1, keepdims=True)
        var = jnp.mean(xc * xc, axis=-1, keepdims=True)
        c_ref[r0:r0 + rb, :] = _silu(xc * lax.rsqrt(var + EPS) * lg_ref[...] + lb_ref[...])

    @pl.when(t == n_chunks - 1)
    def _():
        stout_ref[...] = buf[P0 + L:CONF_PAD + L, :]

    tail = buf[L:L + CONF_PAD, :]
    buf[0:CONF_PAD, :] = tail


def _conf(u, st0, cw, cb, lg, lb, chunk):
    bsz, T, _ = u.shape
    nc = T // chunk
    st = pl.BlockSpec((None, CONF_WIDTH - 1, CONF_CH), lambda b, t: (b, 0, 0))
    return pl.pallas_call(
        functools.partial(_conf_kernel, chunk=chunk, n_chunks=nc), grid=(bsz, nc),
        in_specs=[pl.BlockSpec((None, chunk, 2 * CONF_CH), lambda b, t: (b, t, 0)), st, _resident(cw.shape),
                  _resident(cb.shape), _resident(lg.shape), _resident(lb.shape)],
        out_specs=[pl.BlockSpec((None, chunk, CONF_CH), lambda b, t: (b, t, 0)), st],
        out_shape=[jax.ShapeDtypeStruct((bsz, T, CONF_CH), F32),
                   jax.ShapeDtypeStruct((bsz, CONF_WIDTH - 1, CONF_CH), F32)],
        scratch_shapes=[pltpu.VMEM((CONF_PAD + chunk, CONF_CH), F32),
                        pltpu.VMEM((SUBLANES - 1, CONF_PAD + chunk - SUBLANES, CONF_CH), F32)],
        compiler_params=_params(("arbitrary", "arbitrary"), 24), name="conf_conv")(u, st0, cw, cb, lg, lb)


def _pad_cols(w, width):
    return jnp.pad(w, ((0, 0), (0, width - w.shape[1])))


def _split_cols(w, sizes):
    out, off = [], 0
    for s in sizes:
        out.append(w[:, off:off + s])
        off += s
    return out


def _pad_t(a, bsz, t, t_pad):
    a = a.reshape(bsz, t, a.shape[-1])
    return a if t_pad == t else jnp.pad(a, ((0, 0), (0, t_pad - t), (0, 0)))


def _ceil_to(x, m):
    return -(-x // m) * m


def kernel(x_prompt, x_sample, cache_fox_k, cache_fox_v, cache_fox_logf, page_table, state_ssm, state_ssm_conv, state_gla, state_conf_conv, g_mix, g_mlp, g_final, w_in_ab, ssm_conv_w, ssm_conv_b, ssm_dt_bias, ssm_a_log, ssm_d, ssm_norm_g, gla_gate_w2, gla_gate_b, gla_norm_g, w_out_ab, w_in_cd, fox_b_f, conf_conv_w, conf_conv_b, conf_ln_g, conf_ln_b, w_out_cd, w_mlp_up, w_mlp_down):
    row = lambda v: v.reshape(1, -1)
    wz, wxbc, wdt, wq, wk, wv, wg, wlr = _split_cols(w_in_ab[0], AB_SPLIT)
    n_zx, n_qg = wz.shape[1] + wxbc.shape[1], sum(w.shape[1] for w in (wq, wk, wv, wg))
    off_qg = n_zx + wdt.shape[1]
    w_ab = [w_in_ab[0][:, :n_zx].astype(BF16), w_in_ab[0][:, off_qg:off_qg + n_qg].astype(BF16),
            jnp.concatenate([_pad_cols(wdt, LANES), _pad_cols(wlr, LANES)], axis=1).astype(BF16)]
    cq, ck, cv, cf, cu = _split_cols(w_in_cd[0], CD_SPLIT)
    n_qkv = cq.shape[1] + ck.shape[1] + cv.shape[1]
    w_cd = [w_in_cd[0][:, :n_qkv].astype(BF16), cu.astype(BF16), _pad_cols(cf, LANES).astype(BF16)]
    w_oab = w_out_ab[0].astype(BF16)
    w_ocd = w_out_cd[0].astype(BF16)
    w_up = w_mlp_up.astype(BF16)
    w_dn = w_mlp_down.astype(BF16)
    dtb = _pad_cols(row(ssm_dt_bias[0]), LANES)
    alog = _pad_cols(row(ssm_a_log[0]), LANES)
    dsk = row(jnp.repeat(ssm_d[0], SSM_HEAD_DIM))
    e16 = (jnp.arange(LANES)[:, None] == jnp.arange(SSM_INNER)[None, :] // SSM_HEAD_DIM).astype(BF16)
    w2 = jnp.pad(gla_gate_w2[0], ((0, LANES - GLA_RANK), (0, 0))).astype(BF16)
    bfp = _pad_cols(row(fox_b_f[0]), LANES)
    width = FOX_HEADS * FOX_HEAD_DIM
    n_pool = cache_fox_k.shape[1]
    cache_k = jnp.transpose(cache_fox_k[0], (0, 2, 3, 1)).reshape(n_pool, width, PAGE_SIZE)
    cache_v = jnp.transpose(cache_fox_v[0], (0, 2, 3, 1)).reshape(n_pool, width, PAGE_SIZE)
    lft = jnp.swapaxes(cache_fox_logf[0], 1, 2)

    def trunk(x3, sample):
        bsz, T, _ = x3.shape
        n = bsz * T
        tm = min(n, 512)
        tm_big = min(n, 512)
        x = x3.reshape(n, D_MODEL)
        if sample:
            h0 = state_ssm[0].reshape(bsz, SSM_INNER, SSM_STATE)
            conv0, s0, conf0 = state_ssm_conv[0], state_gla[0].reshape(bsz, GLA_HEADS * GLA_DK, GLA_DV), state_conf_conv[0]
        else:
            h0 = jnp.zeros((bsz, SSM_INNER, SSM_STATE), F32)
            conv0 = jnp.zeros((bsz, SSM_CONV - 1, SSM_CONV_CH), F32)
            s0 = jnp.zeros((bsz, GLA_HEADS * GLA_DK, GLA_DV), F32)
            conf0 = jnp.zeros((bsz, CONF_WIDTH - 1, CONF_CH), F32)

        z, xbc, q, k, v, g, dtp, glr = _norm_proj(x, row(g_mix[0]), w_ab, AB_SECTIONS, tm)
        short = T <= SHORT_CHUNK
        ssd_chunk = SHORT_CHUNK if short else SSD_CHUNK
        gla_rows, gla_chunk = (SHORT_CHUNK, SHORT_CHUNK) if short else (GLA_STEP, GLA_CHUNK)
        nb = SHORT_SEQS if short and bsz % SHORT_SEQS == 0 else 1
        ta = _ceil_to(T, ssd_chunk)
        y, h_new, conv_new = _ssd(_pad_t(z, bsz, T, ta), _pad_t(xbc, bsz, T, ta), _pad_t(dtp, bsz, T, ta), conv0, h0,
                                  ssm_conv_w[0], row(ssm_conv_b[0]), dtb, alog, dsk, row(ssm_norm_g[0]), e16, T,
                                  ssd_chunk, nb)
        tb = _ceil_to(T, gla_rows)
        o, s_new = _gla(_pad_t(q, bsz, T, tb), _pad_t(k, bsz, T, tb), _pad_t(v, bsz, T, tb), _pad_t(g, bsz, T, tb),
                        _pad_t(glr, bsz, T, tb), s0, w2, row(gla_gate_b[0]), row(gla_norm_g[0]), T, gla_rows,
                        gla_chunk, nb)
        y = y[:, :T].reshape(n, SSM_INNER)
        o = o[:, :T].reshape(n, GLA_HEADS * GLA_DV)
        x = _proj_mlp(x, y, o, w_oab, row(g_mlp[0]), w_up[0], w_dn[0], row(g_final), tm_big, False)

        tg = _ceil_to(T, GATE_CHUNK)
        gate_rows = min(tg, GATE_ROWS)
        head_shape = (1, bsz, T, FOX_HEADS, FOX_HEAD_DIM)
        if sample:
            q, k, v, u, f = _norm_proj(x, row(g_mix[1]), w_cd, CD_SECTIONS, tm)
            lf_t, ct = _fox_gate(_pad_t(f, bsz, T, tg), bfp, gate_rows)
            q3, k3, v3 = (a.reshape(bsz, T, width) for a in (q, k, v))
            att = _fox_sample(page_table, q3, k3, v3, ct, cache_k, cache_v, lft)
            k_out, v_out = k.reshape(head_shape), v.reshape(head_shape)
        else:
            q, kb, kt, vt, vtb, u, f = _norm_proj(x, row(g_mix[1]), w_cd, CD_SECTIONS, tm, ("n", "bt", "ts", "n", "n"), T)
            lf_t, ct = _fox_gate(_pad_t(f, bsz, T, tg), bfp, gate_rows)
            att = _fox_prompt(q.reshape(bsz, T, width), kb.reshape(bsz, T, width), vtb, ct)
            to_rows = lambda a: jnp.transpose(a.reshape(1, bsz, FOX_HEADS, FOX_HEAD_DIM, T), (0, 1, 4, 2, 3))
            k_out, v_out = to_rows(kt), to_rows(vt)
        lf = jnp.swapaxes(lf_t, 1, 2)[:, :T]
        cmod, conf_new = _conf(u.reshape(bsz, T, 2 * CONF_CH), conf0, conf_conv_w[0], row(conf_conv_b[0]),
                               row(conf_ln_g[0]), row(conf_ln_b[0]), min(T, CONF_CHUNK))
        x = _proj_mlp(x, att.reshape(n, width), cmod.reshape(n, CONF_CH), w_ocd, row(g_mlp[1]), w_up[1], w_dn[1],
                      row(g_final), tm_big, True)
        return (x.reshape(bsz, T, D_MODEL), h_new.reshape(1, bsz, SSM_HEADS, SSM_HEAD_DIM, SSM_STATE), conv_new[None],
                s_new.reshape(1, bsz, GLA_HEADS, GLA_DK, GLA_DV), k_out, v_out, lf[None], conf_new[None])

    yp, *rest_p = trunk(x_prompt, False)
    ys, *rest_s = trunk(x_sample, True)
    return (yp, ys, *rest_p, *rest_s)
```
